```python
import math
import numpy as np
import jax
import jax.numpy as jnp
from jax import lax

D_MODEL = 1024
BATCH = 32
SEQ = 256
DEPTH = 2
DEC_BATCH = 2
DEC_SEQ = 2048
PAST_LEN = 512

GRID_W = 64
HEAD_DIM = 128
EPS = 1e-6
NEG_INF = -1e30
A_HEADS = D_MODEL // (2 * HEAD_DIM)
A_DK = HEAD_DIM
A_DV = HEAD_DIM
A_W = A_HEADS * HEAD_DIM
A_CONV = 5
DELTA_CHUNK = 64
B_HEADS = D_MODEL // (2 * HEAD_DIM)
B_KV_HEADS = B_HEADS // 2
WINDOW = 128
ATTN_BLOCK = 128
ROPE_BASE = 10000.0
C_WIDTH = D_MODEL
C_GROUPS = 8
C_GROUP_W = C_WIDTH // C_GROUPS
C_CHUNK = 128
PEER_HEADS = 8
N_KEYS = 128
N_EXPERTS = N_KEYS * N_KEYS
PEER_TOPK = 16
D_KEY = 256
PEER_BLOCK = 128
N_AB_LAYERS = (DEPTH + 1) // 2
N_C_LAYERS = DEPTH // 2
AB_SPLITS = (3 * A_W, A_W, 2 * A_HEADS, 2 * A_HEADS, B_HEADS * HEAD_DIM, B_KV_HEADS * HEAD_DIM, B_KV_HEADS * HEAD_DIM)
AB_IN = sum(AB_SPLITS)
AB_OUT = A_W + B_HEADS * HEAD_DIM

kernel_name = 'hybrid_diffusion_prefix_trunk_step'


def rmsnorm(x, g):
    xf = x.astype(jnp.float32)
    y = xf * lax.rsqrt(jnp.mean(xf * xf, axis=-1, keepdims=True) + EPS)
    return (y * g.astype(jnp.float32)).astype(x.dtype)


def l2norm(x):
    return x * lax.rsqrt(jnp.sum(x * x, axis=-1, keepdims=True) + EPS)


def modulation(cvec, w_mod, b_mod):
    m = jax.nn.silu(cvec) @ w_mod + b_mod
    return jnp.split(m[..., None, :], 6, axis=-1)


def modulated_norm(x, g, shift, scale):
    return rmsnorm(x, g) * (1 + scale) + shift


def depthwise_conv_centred(x, w):
    pad = w.shape[0] // 2
    return lax.conv_general_dilated(x, w[:, None, :].astype(x.dtype), window_strides=(1,), padding=[(pad, pad)], dimension_numbers=('NWC', 'WIO', 'NWC'), feature_group_count=x.shape[-1])


def gated_delta_chunked(q, k, v, g, beta, s0):
    B, T, H, dk = q.shape
    dv = v.shape[-1]
    C = DELTA_CHUNK
    N = T // C
    to_chunks = lambda a: a.reshape(B, N, C, H, a.shape[-1]).transpose(1, 0, 3, 2, 4)
    q, k, v = to_chunks(q), to_chunks(k), to_chunks(v)
    g = g.reshape(B, N, C, H).transpose(1, 0, 3, 2)
    beta = beta.reshape(B, N, C, H).transpose(1, 0, 3, 2)
    gc = jnp.cumsum(g, axis=-1)
    tri = jnp.tril(jnp.ones((C, C), dtype=bool))
    strict = jnp.tril(jnp.ones((C, C), dtype=bool), -1)
    diff = gc[..., :, None] - gc[..., None, :]
    gamma = jnp.where(tri, jnp.exp(jnp.where(tri, diff, 0.0)), 0.0)
    kb = k * beta[..., None]
    a_mat = jnp.where(strict, jnp.einsum('nbhid,nbhjd->nbhij', kb, k) * gamma, 0.0)
    rhs = jnp.concatenate([v * beta[..., None], kb * jnp.exp(gc)[..., None]], axis=-1)
    sol = lax.linalg.triangular_solve(a_mat + jnp.eye(C, dtype=a_mat.dtype), rhs, left_side=True, lower=True, unit_diagonal=True)
    u_val, w_dec = sol[..., :dv], sol[..., dv:]
    qk = jnp.einsum('nbhid,nbhjd->nbhij', q, k) * gamma
    q_dec = q * jnp.exp(gc)[..., None]
    k_tail = k * jnp.exp(gc[..., -1:] - gc)[..., None]
    g_last = jnp.exp(gc[..., -1])

    def step(S, inp):
        u_i, w_i, qk_i, qd_i, kt_i, gl_i = inp
        v_new = u_i - jnp.einsum('bhck,bhkv->bhcv', w_i, S)
        o = jnp.einsum('bhck,bhkv->bhcv', qd_i, S) + jnp.einsum('bhij,bhjv->bhiv', qk_i, v_new)
        S = S * gl_i[..., None, None] + jnp.einsum('bhck,bhcv->bhkv', kt_i, v_new)
        return S, o

    S, o = lax.scan(step, s0, (u_val, w_dec, qk, q_dec, k_tail, g_last))
    o = o.transpose(1, 0, 3, 2, 4).reshape(B, T, H, dv)
    return o, S


def delta_mixer(qkv, gate, alpha_logit, beta_logit, conv_w, a_log, dt_bias, onorm_g, s0):
    f32 = jnp.float32
    B, T, _ = qkv.shape
    qkv = jax.nn.silu(depthwise_conv_centred(qkv, conv_w)).astype(f32).reshape(B, T, 3, A_HEADS, HEAD_DIM)
    q = l2norm(qkv[:, :, 0]) * (A_DK ** -0.5)
    k = l2norm(qkv[:, :, 1])
    v = qkv[:, :, 2]
    g = -jnp.exp(a_log.astype(f32)) * jax.nn.softplus(alpha_logit.astype(f32) + dt_bias.astype(f32))
    beta = jax.nn.sigmoid(beta_logit.astype(f32))
    s0 = s0.astype(f32)
    o_f, s_f = gated_delta_chunked(q, k, v, g[:, :, 0], beta[:, :, 0], s0[:, 0])
    rev = lambda a: jnp.flip(a, axis=1)
    o_b, s_b = gated_delta_chunked(rev(q), rev(k), rev(v), rev(g[:, :, 1]), rev(beta[:, :, 1]), s0[:, 1])
    o = o_f + rev(o_b)
    o = o * lax.rsqrt(jnp.mean(o * o, axis=-1, keepdims=True) + EPS) * onorm_g.astype(f32)
    o = o.reshape(B, T, A_W) * jax.nn.silu(gate.astype(f32))
    return o.astype(gate.dtype), jnp.stack([s_f, s_b], axis=1)


def axial_rope(n_tokens):
    rows = n_tokens // GRID_W
    t = jnp.arange(rows * GRID_W)
    row = (t // GRID_W).astype(jnp.float32)
    col = (t % GRID_W).astype(jnp.float32)
    nf = HEAD_DIM // 4
    inv = ROPE_BASE ** (-jnp.arange(nf, dtype=jnp.float32) / nf)
    ang = jnp.concatenate([row[:, None] * inv, col[:, None] * inv], axis=-1)
    return jnp.cos(ang), jnp.sin(ang)


def apply_axial_rope(x, cos, sin):
    B, T, H, hd = x.shape
    nf = hd // 4
    xf = x.astype(jnp.float32).reshape(B, T, H, 2, 2, nf)
    c = cos.reshape(T, 2, nf)[None, :, None]
    s = sin.reshape(T, 2, nf)[None, :, None]
    x1, x2 = xf[..., 0, :], xf[..., 1, :]
    out = jnp.stack([x1 * c - x2 * s, x2 * c + x1 * s], axis=-2)
    return out.reshape(B, T, H, hd).astype(x.dtype)


def gqa_sink_attention(q, k, v, mask, sink):
    B, Q, HQ, hd = q.shape
    HKV = k.shape[2]
    G = HQ // HKV
    qg = q.astype(jnp.float32).reshape(B, Q, HKV, G, hd)
    s = jnp.einsum('bqhgd,bkhd->bhgqk', qg, k.astype(jnp.float32)) * (hd ** -0.5)
    if mask is not None:
        s = jnp.where(mask, s, NEG_INF)
    sink_col = jnp.broadcast_to(sink.astype(jnp.float32).reshape(1, HKV, G, 1, 1), (B, HKV, G, Q, 1))
    p = jax.nn.softmax(jnp.concatenate([s, sink_col], axis=-1), axis=-1)[..., :-1]
    o = jnp.einsum('bhgqk,bkhd->bqhgd', p, v.astype(jnp.float32))
    return o.reshape(B, Q, HQ, hd).astype(q.dtype)


def context_attention(q, k, v, sink):
    B, S, HQ, hd = q.shape
    nb = S // ATTN_BLOCK
    qb = q.reshape(B, nb, ATTN_BLOCK, HQ, hd).swapaxes(0, 1)
    o = lax.map(lambda qi: gqa_sink_attention(qi, k, v, None, sink), qb)
    return o.swapaxes(0, 1).reshape(B, S, HQ, hd)


def latent_attention(q, k, v, k_ctx, v_ctx, sink):
    B, T, HQ, hd = q.shape
    nb = T // ATTN_BLOCK
    span = ATTN_BLOCK + 2 * WINDOW
    pad = ((0, 0), (WINDOW, WINDOW), (0, 0), (0, 0))
    kp = jnp.pad(k, pad)
    vp = jnp.pad(v, pad)
    qb = q.reshape(B, nb, ATTN_BLOCK, HQ, hd).swapaxes(0, 1)
    rel = jnp.arange(span)[None, :] - WINDOW - jnp.arange(ATTN_BLOCK)[:, None]
    ctx_ok = jnp.ones((ATTN_BLOCK, k_ctx.shape[1]), dtype=bool)

    def block(args):
        i, qi = args
        start = i * ATTN_BLOCK
        kb = lax.dynamic_slice_in_dim(kp, start, span, axis=1)
        vb = lax.dynamic_slice_in_dim(vp, start, span, axis=1)
        kpos = start - WINDOW + jnp.arange(span)
        band_ok = (jnp.abs(rel) <= WINDOW) & ((kpos >= 0) & (kpos < T))[None, :]
        mask = jnp.concatenate([ctx_ok, band_ok], axis=1)
        keys = jnp.concatenate([k_ctx, kb], axis=1)
        vals = jnp.concatenate([v_ctx, vb], axis=1)
        return gqa_sink_attention(qi, keys, vals, mask, sink)

    o = lax.map(block, (jnp.arange(nb), qb))
    return o.swapaxes(0, 1).reshape(B, T, HQ, hd)


def ab_mixer(h, w_in, conv_w, a_log, dt_bias, onorm_g, sink, w_out, s0, kv_ctx, rope):
    B, T, _ = h.shape
    cuts = np.cumsum(AB_SPLITS)[:-1].tolist()
    qkv_a, gate_a, alpha_a, beta_a, q_b, k_b, v_b = jnp.split(h @ w_in, cuts, axis=-1)
    o_a, s_fin = delta_mixer(qkv_a, gate_a, alpha_a.reshape(B, T, 2, A_HEADS), beta_a.reshape(B, T, 2, A_HEADS), conv_w, a_log, dt_bias, onorm_g, s0)
    q_b = q_b.reshape(B, T, B_HEADS, HEAD_DIM)
    k_b = k_b.reshape(B, T, B_KV_HEADS, HEAD_DIM)
    v_b = v_b.reshape(B, T, B_KV_HEADS, HEAD_DIM)
    if rope is None:
        o_b = context_attention(q_b, k_b, v_b, sink)
    else:
        cos, sin = rope
        o_b = latent_attention(apply_axial_rope(q_b, cos, sin), apply_axial_rope(k_b, cos, sin), v_b, kv_ctx[0], kv_ctx[1], sink)
    out = jnp.concatenate([o_a, o_b.reshape(B, T, B_HEADS * HEAD_DIM)], axis=-1) @ w_out
    return out, k_b, v_b, s_fin.astype(h.dtype)


def chunk_mlp_mixer(h, w_in, v_norm_g, w_s, b_s, w_out):
    B, T, _ = h.shape
    nc = T // C_CHUNK
    u, v = jnp.split(jax.nn.gelu(h @ w_in), 2, axis=-1)
    v = rmsnorm(v, v_norm_g).reshape(B, nc, C_CHUNK, C_GROUPS, C_GROUP_W)
    mixed = jnp.einsum('gij,bnjgc->bnigc', w_s, v) + b_s.T[None, None, :, :, None]
    return (u * mixed.reshape(B, T, C_WIDTH)) @ w_out


def peer_ffn(h, w_q, sub_keys, e_down, e_up):
    B, T, D = h.shape
    nb = (B * T) // PEER_BLOCK
    xb = h.reshape(nb, PEER_BLOCK, D)
    keys = sub_keys.astype(jnp.float32)
    kk = PEER_TOPK * PEER_TOPK

    def block(xi):
        q = (xi @ w_q).astype(jnp.float32).reshape(PEER_BLOCK, PEER_HEADS, 2, D_KEY // 2)
        s = jnp.einsum('thpd,pnd->thpn', q, keys)
        sv, si = lax.top_k(s, PEER_TOPK)
        cand_s = (sv[:, :, 0, :, None] + sv[:, :, 1, None, :]).reshape(PEER_BLOCK, PEER_HEADS, kk)
        cand_i = (si[:, :, 0, :, None] * N_KEYS + si[:, :, 1, None, :]).reshape(PEER_BLOCK, PEER_HEADS, kk)
        top_s, top_j = lax.top_k(cand_s, PEER_TOPK)
        idx = jnp.take_along_axis(cand_i, top_j, axis=-1)
        gates = jax.nn.softmax(top_s, axis=-1)
        act = jax.nn.gelu(jnp.einsum('td,thed->the', xi, e_down[idx]).astype(jnp.float32))
        return jnp.einsum('the,thed->td', (gates * act).astype(xi.dtype), e_up[idx])

    return lax.map(block, xb).reshape(B, T, D)


def setup_inputs(seed: int = 0) -> dict:
    key = jax.random.key(seed)
    ks = jax.random.split(key, 32)
    f32 = jnp.float32
    D = D_MODEL

    def nrm(k, shape, scale):
        return jax.random.normal(k, shape, f32) * scale

    dt = jnp.exp(jax.random.uniform(ks[9], (N_AB_LAYERS, 2, A_HEADS), f32, math.log(1e-3), math.log(1e-1)))
    return {
        'x_prompt': nrm(ks[0], (BATCH, SEQ, D), 1.0),
        'x_sample': nrm(ks[1], (DEC_BATCH, DEC_SEQ, D), 1.0),
        'cache_attn_k': nrm(ks[2], (DEC_BATCH, N_AB_LAYERS, PAST_LEN, B_KV_HEADS, HEAD_DIM), 1.0),
        'cache_attn_v': nrm(ks[3], (DEC_BATCH, N_AB_LAYERS, PAST_LEN, B_KV_HEADS, HEAD_DIM), 1.0),
        'state_delta': nrm(ks[4], (DEC_BATCH, N_AB_LAYERS, 2, A_HEADS, A_DK, A_DV), 0.5),
        'c': nrm(ks[5], (DEC_BATCH, D), 1.0),
        'c_ctx': nrm(ks[6], (D,), 1.0),
        'w_mod': nrm(ks[7], (DEPTH, D, 6 * D), 0.5 * D ** -0.5),
        'b_mod': nrm(ks[8], (DEPTH, 6 * D), 0.01),
        'norm_g': 1.0 + nrm(ks[10], (DEPTH, 2, D), 0.01),
        'w_in_ab': nrm(ks[11], (N_AB_LAYERS, D, AB_IN), D ** -0.5),
        'conv_ab': nrm(ks[12], (N_AB_LAYERS, A_CONV, 3 * A_W), A_CONV ** -0.5),
        'a_log_ab': jnp.log(jax.random.uniform(ks[13], (N_AB_LAYERS, 2, A_HEADS), f32, 1.0, 16.0)),
        'dt_bias_ab': dt + jnp.log(-jnp.expm1(-dt)),
        'onorm_ab': 1.0 + nrm(ks[14], (N_AB_LAYERS, A_DV), 0.01),
        'sink_ab': nrm(ks[15], (N_AB_LAYERS, B_HEADS), 1.0),
        'w_out_ab': nrm(ks[16], (N_AB_LAYERS, AB_OUT, D), AB_OUT ** -0.5),
        'w_in_c': nrm(ks[17], (N_C_LAYERS, D, 2 * C_WIDTH), D ** -0.5),
        'vnorm_c': 1.0 + nrm(ks[18], (N_C_LAYERS, C_WIDTH), 0.01),
        'w_s_c': nrm(ks[19], (N_C_LAYERS, C_GROUPS, C_CHUNK, C_CHUNK), C_CHUNK ** -0.5),
        'b_s_c': nrm(ks[20], (N_C_LAYERS, C_GROUPS, C_CHUNK), 0.02),
        'w_out_c': nrm(ks[21], (N_C_LAYERS, C_WIDTH, D), C_WIDTH ** -0.5),
        'peer_wq': nrm(ks[22], (DEPTH, D, PEER_HEADS * D_KEY), D ** -0.5),
        'peer_keys': nrm(ks[23], (DEPTH, 2, N_KEYS, D_KEY // 2), (D_KEY // 2) ** -0.5),
        'peer_down': nrm(ks[24], (DEPTH, N_EXPERTS, D), D ** -0.5),
        'peer_up': nrm(ks[25], (DEPTH, N_EXPERTS, D), PEER_HEADS ** -0.5),
        'final_norm_g': 1.0 + nrm(ks[26], (D,), 0.01),
    }


def reference(x_prompt, x_sample, cache_attn_k, cache_attn_v, state_delta, c, c_ctx,
              w_mod, b_mod, norm_g, w_in_ab, conv_ab, a_log_ab, dt_bias_ab, onorm_ab, sink_ab, w_out_ab,
              w_in_c, vnorm_c, w_s_c, b_s_c, w_out_c,
              peer_wq, peer_keys, peer_down, peer_up, final_norm_g):
    xp = x_prompt
    xs = x_sample
    rope = axial_rope(x_sample.shape[1])
    zero_state = jnp.zeros((x_prompt.shape[0], 2, A_HEADS, A_DK, A_DV), jnp.float32)
    new_k, new_v, new_s = [], [], []
    for layer in range(DEPTH):
        mp = modulation(c_ctx, w_mod[layer], b_mod[layer])
        ms = modulation(c, w_mod[layer], b_mod[layer])
        hp = modulated_norm(xp, norm_g[layer, 0], mp[0], mp[1])
        hs = modulated_norm(xs, norm_g[layer, 0], ms[0], ms[1])
        li = layer // 2
        if layer % 2 == 0:
            wts = (w_in_ab[li], conv_ab[li], a_log_ab[li], dt_bias_ab[li], onorm_ab[li], sink_ab[li], w_out_ab[li])
            op, kc, vc, sc = ab_mixer(hp, *wts, zero_state, None, None)
            os_, _, _, _ = ab_mixer(hs, *wts, state_delta[:, li], (cache_attn_k[:, li], cache_attn_v[:, li]), rope)
            new_k.append(kc)
            new_v.append(vc)
            new_s.append(sc)
        else:
            wts = (w_in_c[li], vnorm_c[li], w_s_c[li], b_s_c[li], w_out_c[li])
            op = chunk_mlp_mixer(hp, *wts)
            os_ = chunk_mlp_mixer(hs, *wts)
        xp = xp + mp[2] * op
        xs = xs + ms[2] * os_
        hp = modulated_norm(xp, norm_g[layer, 1], mp[3], mp[4])
        hs = modulated_norm(xs, norm_g[layer, 1], ms[3], ms[4])
        pw = (peer_wq[layer], peer_keys[layer], peer_down[layer], peer_up[layer])
        xp = xp + mp[5] * peer_ffn(hp, *pw)
        xs = xs + ms[5] * peer_ffn(hs, *pw)
    y_prompt = rmsnorm(xp, final_norm_g)
    y_sample = rmsnorm(xs, final_norm_g)
    new_attn_k = jnp.stack(new_k, axis=1)
    new_attn_v = jnp.stack(new_v, axis=1)
    new_state_delta = jnp.stack(new_s, axis=1)
    return (y_prompt, y_sample, new_attn_k, new_attn_v, new_state_delta)
```

```python
import functools
import math

import numpy as np
import jax
import jax.numpy as jnp
from jax import lax
from jax.experimental import pallas as pl
from jax.experimental.pallas import tpu as pltpu

D_MODEL = 1024
DEPTH = 2
GRID_W = 64
HEAD_DIM = 128
EPS = 1e-6
NEG_INF = -1e30
A_HEADS = D_MODEL // (2 * HEAD_DIM)
A_DK = HEAD_DIM
A_DV = HEAD_DIM
A_W = A_HEADS * HEAD_DIM
A_CONV = 5
DELTA_CHUNK = 64
B_HEADS = D_MODEL // (2 * HEAD_DIM)
B_KV_HEADS = B_HEADS // 2
WINDOW = 128
ATTN_BLOCK = 128
ROPE_BASE = 10000.0
C_WIDTH = D_MODEL
C_GROUPS = 8
C_GROUP_W = C_WIDTH // C_GROUPS
C_CHUNK = 128
PEER_HEADS = 8
N_KEYS = 128
PEER_TOPK = 16
D_KEY = 256
PEER_BLOCK = 128
AB_SPLITS = (3 * A_W, A_W, 2 * A_HEADS, 2 * A_HEADS, B_HEADS * HEAD_DIM, B_KV_HEADS * HEAD_DIM, B_KV_HEADS * HEAD_DIM)

LANE = 128
VMEM_LIMIT = 48 * 1024 * 1024


def _mm_kernel(x_ref, w_ref, o_ref):
    o_ref[...] = jnp.dot(x_ref[...].astype(jnp.bfloat16), w_ref[...], preferred_element_type=jnp.float32)


def mm(x, w, tm=512):
    M, K = x.shape
    N = w.shape[1]
    n_pad = (-N) % LANE
    wb = w.astype(jnp.bfloat16)
    if n_pad:
        wb = jnp.pad(wb, ((0, 0), (0, n_pad)))
    Np = N + n_pad
    tn = LANE
    for cand in range(LANE, 1024 + 1, LANE):
        if Np % cand == 0:
            tn = cand
    tm = min(tm, M)
    out = pl.pallas_call(
        _mm_kernel,
        grid=(Np // tn, M // tm),
        in_specs=[pl.BlockSpec((tm, K), lambda j, i: (i, 0)), pl.BlockSpec((K, tn), lambda j, i: (0, j))],
        out_specs=pl.BlockSpec((tm, tn), lambda j, i: (i, j)),
        out_shape=jax.ShapeDtypeStruct((M, Np), jnp.float32),
        compiler_params=pltpu.CompilerParams(dimension_semantics=("arbitrary", "arbitrary"), vmem_limit_bytes=VMEM_LIMIT),
        name="mm",
    )(x, wb)
    return out[:, :N] if n_pad else out


def mm3(x, w):
    B, T, K = x.shape
    return mm(x.reshape(B * T, K), w).reshape(B, T, w.shape[1])


def rmsnorm(x, g):
    xf = x.astype(jnp.float32)
    y = xf * lax.rsqrt(jnp.mean(xf * xf, axis=-1, keepdims=True) + EPS)
    return (y * g.astype(jnp.float32)).astype(x.dtype)


def l2norm(x):
    return x * lax.rsqrt(jnp.sum(x * x, axis=-1, keepdims=True) + EPS)


def modulation(cvec, w_mod, b_mod):
    m = jax.nn.silu(cvec) @ w_mod + b_mod
    return jnp.split(m[..., None, :], 6, axis=-1)


def modulated_norm(x, g, shift, scale):
    return rmsnorm(x, g) * (1 + scale) + shift


def depthwise_conv_centred(x, w):
    pad = w.shape[0] // 2
    return lax.conv_general_dilated(x, w[:, None, :].astype(x.dtype), window_strides=(1,), padding=[(pad, pad)], dimension_numbers=('NWC', 'WIO', 'NWC'), feature_group_count=x.shape[-1])


def gated_delta_chunked(q, k, v, g, beta, s0):
    B, T, H, dk = q.shape
    dv = v.shape[-1]
    C = DELTA_CHUNK
    N = T // C
    to_chunks = lambda a: a.reshape(B, N, C, H, a.shape[-1]).transpose(1, 0, 3, 2, 4)
    q, k, v = to_chunks(q), to_chunks(k), to_chunks(v)
    g = g.reshape(B, N, C, H).transpose(1, 0, 3, 2)
    beta = beta.reshape(B, N, C, H).transpose(1, 0, 3, 2)
    gc = jnp.cumsum(g, axis=-1)
    tri = jnp.tril(jnp.ones((C, C), dtype=bool))
    strict = jnp.tril(jnp.ones((C, C), dtype=bool), -1)
    diff = gc[..., :, None] - gc[..., None, :]
    gamma = jnp.where(tri, jnp.exp(jnp.where(tri, diff, 0.0)), 0.0)
    kb = k * beta[..., None]
    a_mat = jnp.where(strict, jnp.einsum('nbhid,nbhjd->nbhij', kb, k) * gamma, 0.0)
    rhs = jnp.concatenate([v * beta[..., None], kb * jnp.exp(gc)[..., None]], axis=-1)
    sol = lax.linalg.triangular_solve(a_mat + jnp.eye(C, dtype=a_mat.dtype), rhs, left_side=True, lower=True, unit_diagonal=True)
    u_val, w_dec = sol[..., :dv], sol[..., dv:]
    qk = jnp.einsum('nbhid,nbhjd->nbhij', q, k) * gamma
    q_dec = q * jnp.exp(gc)[..., None]
    k_tail = k * jnp.exp(gc[..., -1:] - gc)[..., None]
    g_last = jnp.exp(gc[..., -1])

    def step(S, inp):
        u_i, w_i, qk_i, qd_i, kt_i, gl_i = inp
        v_new = u_i - jnp.einsum('bhck,bhkv->bhcv', w_i, S)
        o = jnp.einsum('bhck,bhkv->bhcv', qd_i, S) + jnp.einsum('bhij,bhjv->bhiv', qk_i, v_new)
        S = S * gl_i[..., None, None] + jnp.einsum('bhck,bhcv->bhkv', kt_i, v_new)
        return S, o

    S, o = lax.scan(step, s0, (u_val, w_dec, qk, q_dec, k_tail, g_last))
    o = o.transpose(1, 0, 3, 2, 4).reshape(B, T, H, dv)
    return o, S


def delta_mixer(qkv, gate, alpha_logit, beta_logit, conv_w, a_log, dt_bias, onorm_g, s0):
    f32 = jnp.float32
    B, T, _ = qkv.shape
    qkv = jax.nn.silu(depthwise_conv_centred(qkv, conv_w)).astype(f32).reshape(B, T, 3, A_HEADS, HEAD_DIM)
    q = l2norm(qkv[:, :, 0]) * (A_DK ** -0.5)
    k = l2norm(qkv[:, :, 1])
    v = qkv[:, :, 2]
    g = -jnp.exp(a_log.astype(f32)) * jax.nn.softplus(alpha_logit.astype(f32) + dt_bias.astype(f32))
    beta = jax.nn.sigmoid(beta_logit.astype(f32))
    s0 = s0.astype(f32)
    o_f, s_f = gated_delta_chunked(q, k, v, g[:, :, 0], beta[:, :, 0], s0[:, 0])
    rev = lambda a: jnp.flip(a, axis=1)
    o_b, s_b = gated_delta_chunked(rev(q), rev(k), rev(v), rev(g[:, :, 1]), rev(beta[:, :, 1]), s0[:, 1])
    o = o_f + rev(o_b)
    o = o * lax.rsqrt(jnp.mean(o * o, axis=-1, keepdims=True) + EPS) * onorm_g.astype(f32)
    o = o.reshape(B, T, A_W) * jax.nn.silu(gate.astype(f32))
    return o.astype(gate.dtype), jnp.stack([s_f, s_b], axis=1)


def axial_rope(n_tokens):
    rows = n_tokens // GRID_W
    t = jnp.arange(rows * GRID_W)
    row = (t // GRID_W).astype(jnp.float32)
    col = (t % GRID_W).astype(jnp.float32)
    nf = HEAD_DIM // 4
    inv = ROPE_BASE ** (-jnp.arange(nf, dtype=jnp.float32) / nf)
    ang = jnp.concatenate([row[:, None] * inv, col[:, None] * inv], axis=-1)
    return jnp.cos(ang), jnp.sin(ang)


def apply_axial_rope(x, cos, sin):
    B, T, H, hd = x.shape
    nf = hd // 4
    xf = x.astype(jnp.float32).reshape(B, T, H, 2, 2, nf)
    c = cos.reshape(T, 2, nf)[None, :, None]
    s = sin.reshape(T, 2, nf)[None, :, None]
    x1, x2 = xf[..., 0, :], xf[..., 1, :]
    out = jnp.stack([x1 * c - x2 * s, x2 * c + x1 * s], axis=-2)
    return out.reshape(B, T, H, hd).astype(x.dtype)


def gqa_sink_attention(q, k, v, mask, sink):
    B, Q, HQ, hd = q.shape
    HKV = k.shape[2]
    G = HQ // HKV
    qg = q.astype(jnp.float32).reshape(B, Q, HKV, G, hd)
    s = jnp.einsum('bqhgd,bkhd->bhgqk', qg, k.astype(jnp.float32)) * (hd ** -0.5)
    if mask is not None:
        s = jnp.where(mask, s, NEG_INF)
    sink_col = jnp.broadcast_to(sink.astype(jnp.float32).reshape(1, HKV, G, 1, 1), (B, HKV, G, Q, 1))
    p = jax.nn.softmax(jnp.concatenate([s, sink_col], axis=-1), axis=-1)[..., :-1]
    o = jnp.einsum('bhgqk,bkhd->bqhgd', p, v.astype(jnp.float32))
    return o.reshape(B, Q, HQ, hd).astype(q.dtype)


def context_attention(q, k, v, sink):
    B, S, HQ, hd = q.shape
    nb = S // ATTN_BLOCK
    qb = q.reshape(B, nb, ATTN_BLOCK, HQ, hd).swapaxes(0, 1)
    o = lax.map(lambda qi: gqa_sink_attention(qi, k, v, None, sink), qb)
    return o.swapaxes(0, 1).reshape(B, S, HQ, hd)


def latent_attention(q, k, v, k_ctx, v_ctx, sink):
    B, T, HQ, hd = q.shape
    nb = T // ATTN_BLOCK
    span = ATTN_BLOCK + 2 * WINDOW
    pad = ((0, 0), (WINDOW, WINDOW), (0, 0), (0, 0))
    kp = jnp.pad(k, pad)
    vp = jnp.pad(v, pad)
    qb = q.reshape(B, nb, ATTN_BLOCK, HQ, hd).swapaxes(0, 1)
    rel = jnp.arange(span)[None, :] - WINDOW - jnp.arange(ATTN_BLOCK)[:, None]
    ctx_ok = jnp.ones((ATTN_BLOCK, k_ctx.shape[1]), dtype=bool)

    def block(args):
        i, qi = args
        start = i * ATTN_BLOCK
        kb = lax.dynamic_slice_in_dim(kp, start, span, axis=1)
        vb = lax.dynamic_slice_in_dim(vp, start, span, axis=1)
        kpos = start - WINDOW + jnp.arange(span)
        band_ok = (jnp.abs(rel) <= WINDOW) & ((kpos >= 0) & (kpos < T))[None, :]
        mask = jnp.concatenate([ctx_ok, band_ok], axis=1)
        keys = jnp.concatenate([k_ctx, kb], axis=1)
        vals = jnp.concatenate([v_ctx, vb], axis=1)
        return gqa_sink_attention(qi, keys, vals, mask, sink)

    o = lax.map(block, (jnp.arange(nb), qb))
    return o.swapaxes(0, 1).reshape(B, T, HQ, hd)


def ab_mixer(h, w_in, conv_w, a_log, dt_bias, onorm_g, sink, w_out, s0, kv_ctx, rope):
    B, T, _ = h.shape
    cuts = np.cumsum(AB_SPLITS)[:-1].tolist()
    qkv_a, gate_a, alpha_a, beta_a, q_b, k_b, v_b = jnp.split(mm3(h, w_in), cuts, axis=-1)
    o_a, s_fin = delta_mixer(qkv_a, gate_a, alpha_a.reshape(B, T, 2, A_HEADS), beta_a.reshape(B, T, 2, A_HEADS), conv_w, a_log, dt_bias, onorm_g, s0)
    q_b = q_b.reshape(B, T, B_HEADS, HEAD_DIM)
    k_b = k_b.reshape(B, T, B_KV_HEADS, HEAD_DIM)
    v_b = v_b.reshape(B, T, B_KV_HEADS, HEAD_DIM)
    if rope is None:
        o_b = context_attention(q_b, k_b, v_b, sink)
    else:
        cos, sin = rope
        o_b = latent_attention(apply_axial_rope(q_b, cos, sin), apply_axial_rope(k_b, cos, sin), v_b, kv_ctx[0], kv_ctx[1], sink)
    out = mm3(jnp.concatenate([o_a, o_b.reshape(B, T, B_HEADS * HEAD_DIM)], axis=-1), w_out)
    return out, k_b, v_b, s_fin.astype(h.dtype)


def chunk_mlp_mixer(h, w_in, v_norm_g, w_s, b_s, w_out):
    B, T, _ = h.shape
    nc = T // C_CHUNK
    u, v = jnp.split(jax.nn.gelu(mm3(h, w_in)), 2, axis=-1)
    v = rmsnorm(v, v_norm_g).reshape(B, nc, C_CHUNK, C_GROUPS, C_GROUP_W)
    mixed = jnp.einsum('gij,bnjgc->bnigc', w_s, v) + b_s.T[None, None, :, :, None]
    return mm3(u * mixed.reshape(B, T, C_WIDTH), w_out)


def peer_ffn(h, w_q, sub_keys, e_down, e_up):
    B, T, D = h.shape
    nb = (B * T) // PEER_BLOCK
    xb = h.reshape(nb, PEER_BLOCK, D)
    qb = mm(h.reshape(B * T, D), w_q).reshape(nb, PEER_BLOCK, PEER_HEADS * D_KEY)
    keys = sub_keys.astype(jnp.float32)
    kk = PEER_TOPK * PEER_TOPK

    def block(args):
        xi, qi = args
        q = qi.astype(jnp.float32).reshape(PEER_BLOCK, PEER_HEADS, 2, D_KEY // 2)
        s = jnp.einsum('thpd,pnd->thpn', q, keys)
        sv, si = lax.top_k(s, PEER_TOPK)
        cand_s = (sv[:, :, 0, :, None] + sv[:, :, 1, None, :]).reshape(PEER_BLOCK, PEER_HEADS, kk)
        cand_i = (si[:, :, 0, :, None] * N_KEYS + si[:, :, 1, None, :]).reshape(PEER_BLOCK, PEER_HEADS, kk)
        top_s, top_j = lax.top_k(cand_s, PEER_TOPK)
        idx = jnp.take_along_axis(cand_i, top_j, axis=-1)
        gates = jax.nn.softmax(top_s, axis=-1)
        act = jax.nn.gelu(jnp.einsum('td,thed->the', xi, e_down[idx]).astype(jnp.float32))
        return jnp.einsum('the,thed->td', (gates * act).astype(xi.dtype), e_up[idx])

    return lax.map(block, (xb, qb)).reshape(B, T, D)


def kernel(x_prompt, x_sample, cache_attn_k, cache_attn_v, state_delta, c, c_ctx, w_mod, b_mod, norm_g, w_in_ab, conv_ab, a_log_ab, dt_bias_ab, onorm_ab, sink_ab, w_out_ab, w_in_c, vnorm_c, w_s_c, b_s_c, w_out_c, peer_wq, peer_keys, peer_down, peer_up, final_norm_g):
    xp = x_prompt
    xs = x_sample
    rope = axial_rope(x_sample.shape[1])
    zero_state = jnp.zeros((x_prompt.shape[0], 2, A_HEADS, A_DK, A_DV), jnp.float32)
    new_k, new_v, new_s = [], [], []
    for layer in range(DEPTH):
        mp = modulation(c_ctx, w_mod[layer], b_mod[layer])
        ms = modulation(c, w_mod[layer], b_mod[layer])
        hp = modulated_norm(xp, norm_g[layer, 0], mp[0], mp[1])
        hs = modulated_norm(xs, norm_g[layer, 0], ms[0], ms[1])
        li = layer // 2
        if layer % 2 == 0:
            wts = (w_in_ab[li], conv_ab[li], a_log_ab[li], dt_bias_ab[li], onorm_ab[li], sink_ab[li], w_out_ab[li])
            op, kc, vc, sc = ab_mixer(hp, *wts, zero_state, None, None)
            os_, _, _, _ = ab_mixer(hs, *wts, state_delta[:, li], (cache_attn_k[:, li], cache_attn_v[:, li]), rope)
            new_k.append(kc)
            new_v.append(vc)
            new_s.append(sc)
        else:
            wts = (w_in_c[li], vnorm_c[li], w_s_c[li], b_s_c[li], w_out_c[li])
            op = chunk_mlp_mixer(hp, *wts)
            os_ = chunk_mlp_mixer(hs, *wts)
        xp = xp + mp[2] * op
        xs = xs + ms[2] * os_
        hp = modulated_norm(xp, norm_g[layer, 1], mp[3], mp[4])
        hs = modulated_norm(xs, norm_g[layer, 1], ms[3], ms[4])
        pw = (peer_wq[layer], peer_keys[layer], peer_down[layer], peer_up[layer])
        xp = xp + mp[5] * peer_ffn(hp, *pw)
        xs = xs + ms[5] * peer_ffn(hs, *pw)
    y_prompt = rmsnorm(xp, final_norm_g)
    y_sample = rmsnorm(xs, final_norm_g)
    new_attn_k = jnp.stack(new_k, axis=1)
    new_attn_v = jnp.stack(new_v, axis=1)
    new_state_delta = jnp.stack(new_s, axis=1)
    return (y_prompt, y_sample, new_attn_k, new_attn_v, new_state_delta)
```

```python
import functools
import math

import numpy as np
import jax
import jax.numpy as jnp
from jax import lax
from jax.experimental import pallas as pl
from jax.experimental.pallas import tpu as pltpu

D_MODEL = 1024
DEPTH = 2
GRID_W = 64
HEAD_DIM = 128
EPS = 1e-6
NEG_INF = -1e30
A_HEADS = D_MODEL // (2 * HEAD_DIM)
A_DK = HEAD_DIM
A_DV = HEAD_DIM
A_W = A_HEADS * HEAD_DIM
A_CONV = 5
DELTA_CHUNK = 64
B_HEADS = D_MODEL // (2 * HEAD_DIM)
B_KV_HEADS = B_HEADS // 2
WINDOW = 128
ATTN_BLOCK = 128
ROPE_BASE = 10000.0
C_WIDTH = D_MODEL
C_GROUPS = 8
C_GROUP_W = C_WIDTH // C_GROUPS
C_CHUNK = 128
PEER_HEADS = 8
N_KEYS = 128
PEER_TOPK = 16
D_KEY = 256
PEER_BLOCK = 128
AB_SPLITS = (3 * A_W, A_W, 2 * A_HEADS, 2 * A_HEADS, B_HEADS * HEAD_DIM, B_KV_HEADS * HEAD_DIM, B_KV_HEADS * HEAD_DIM)

LANE = 128
VMEM_LIMIT = 48 * 1024 * 1024


def _mm_kernel(x_ref, w_ref, o_ref):
    o_ref[...] = jnp.dot(x_ref[...].astype(jnp.bfloat16), w_ref[...], preferred_element_type=jnp.float32)


def mm(x, w, tm=512):
    M, K = x.shape
    N = w.shape[1]
    n_pad = (-N) % LANE
    wb = w.astype(jnp.bfloat16)
    if n_pad:
        wb = jnp.pad(wb, ((0, 0), (0, n_pad)))
    Np = N + n_pad
    tn = LANE
    for cand in range(LANE, 1024 + 1, LANE):
        if Np % cand == 0:
            tn = cand
    tm = min(tm, M)
    out = pl.pallas_call(
        _mm_kernel,
        grid=(Np // tn, M // tm),
        in_specs=[pl.BlockSpec((tm, K), lambda j, i: (i, 0)), pl.BlockSpec((K, tn), lambda j, i: (0, j))],
        out_specs=pl.BlockSpec((tm, tn), lambda j, i: (i, j)),
        out_shape=jax.ShapeDtypeStruct((M, Np), jnp.float32),
        compiler_params=pltpu.CompilerParams(dimension_semantics=("arbitrary", "arbitrary"), vmem_limit_bytes=VMEM_LIMIT),
        name="mm",
    )(x, wb)
    return out[:, :N] if n_pad else out


def mm3(x, w):
    B, T, K = x.shape
    return mm(x.reshape(B * T, K), w).reshape(B, T, w.shape[1])


def rmsnorm(x, g):
    xf = x.astype(jnp.float32)
    y = xf * lax.rsqrt(jnp.mean(xf * xf, axis=-1, keepdims=True) + EPS)
    return (y * g.astype(jnp.float32)).astype(x.dtype)


def l2norm(x):
    return x * lax.rsqrt(jnp.sum(x * x, axis=-1, keepdims=True) + EPS)


def modulation(cvec, w_mod, b_mod):
    m = jax.nn.silu(cvec) @ w_mod + b_mod
    return jnp.split(m[..., None, :], 6, axis=-1)


def modulated_norm(x, g, shift, scale):
    return rmsnorm(x, g) * (1 + scale) + shift


def depthwise_conv_centred(x, w):
    pad = w.shape[0] // 2
    return lax.conv_general_dilated(x, w[:, None, :].astype(x.dtype), window_strides=(1,), padding=[(pad, pad)], dimension_numbers=('NWC', 'WIO', 'NWC'), feature_group_count=x.shape[-1])


def gated_delta_chunked(q, k, v, g, beta, s0):
    B, T, H, dk = q.shape
    dv = v.shape[-1]
    C = DELTA_CHUNK
    N = T // C
    to_chunks = lambda a: a.reshape(B, N, C, H, a.shape[-1]).transpose(1, 0, 3, 2, 4)
    q, k, v = to_chunks(q), to_chunks(k), to_chunks(v)
    g = g.reshape(B, N, C, H).transpose(1, 0, 3, 2)
    beta = beta.reshape(B, N, C, H).transpose(1, 0, 3, 2)
    gc = jnp.cumsum(g, axis=-1)
    tri = jnp.tril(jnp.ones((C, C), dtype=bool))
    strict = jnp.tril(jnp.ones((C, C), dtype=bool), -1)
    diff = gc[..., :, None] - gc[..., None, :]
    gamma = jnp.where(tri, jnp.exp(jnp.where(tri, diff, 0.0)), 0.0)
    kb = k * beta[..., None]
    a_mat = jnp.where(strict, jnp.einsum('nbhid,nbhjd->nbhij', kb, k) * gamma, 0.0)
    rhs = jnp.concatenate([v * beta[..., None], kb * jnp.exp(gc)[..., None]], axis=-1)
    sol = lax.linalg.triangular_solve(a_mat + jnp.eye(C, dtype=a_mat.dtype), rhs, left_side=True, lower=True, unit_diagonal=True)
    u_val, w_dec = sol[..., :dv], sol[..., dv:]
    qk = jnp.einsum('nbhid,nbhjd->nbhij', q, k) * gamma
    q_dec = q * jnp.exp(gc)[..., None]
    k_tail = k * jnp.exp(gc[..., -1:] - gc)[..., None]
    g_last = jnp.exp(gc[..., -1])

    def step(S, inp):
        u_i, w_i, qk_i, qd_i, kt_i, gl_i = inp
        v_new = u_i - jnp.einsum('bhck,bhkv->bhcv', w_i, S)
        o = jnp.einsum('bhck,bhkv->bhcv', qd_i, S) + jnp.einsum('bhij,bhjv->bhiv', qk_i, v_new)
        S = S * gl_i[..., None, None] + jnp.einsum('bhck,bhcv->bhkv', kt_i, v_new)
        return S, o

    S, o = lax.scan(step, s0, (u_val, w_dec, qk, q_dec, k_tail, g_last))
    o = o.transpose(1, 0, 3, 2, 4).reshape(B, T, H, dv)
    return o, S


def delta_mixer(qkv, gate, alpha_logit, beta_logit, conv_w, a_log, dt_bias, onorm_g, s0):
    f32 = jnp.float32
    B, T, _ = qkv.shape
    qkv = jax.nn.silu(depthwise_conv_centred(qkv, conv_w)).astype(f32).reshape(B, T, 3, A_HEADS, HEAD_DIM)
    q = l2norm(qkv[:, :, 0]) * (A_DK ** -0.5)
    k = l2norm(qkv[:, :, 1])
    v = qkv[:, :, 2]
    g = -jnp.exp(a_log.astype(f32)) * jax.nn.softplus(alpha_logit.astype(f32) + dt_bias.astype(f32))
    beta = jax.nn.sigmoid(beta_logit.astype(f32))
    s0 = s0.astype(f32)
    o_f, s_f = gated_delta_chunked(q, k, v, g[:, :, 0], beta[:, :, 0], s0[:, 0])
    rev = lambda a: jnp.flip(a, axis=1)
    o_b, s_b = gated_delta_chunked(rev(q), rev(k), rev(v), rev(g[:, :, 1]), rev(beta[:, :, 1]), s0[:, 1])
    o = o_f + rev(o_b)
    o = o * lax.rsqrt(jnp.mean(o * o, axis=-1, keepdims=True) + EPS) * onorm_g.astype(f32)
    o = o.reshape(B, T, A_W) * jax.nn.silu(gate.astype(f32))
    return o.astype(gate.dtype), jnp.stack([s_f, s_b], axis=1)


def axial_rope(n_tokens):
    rows = n_tokens // GRID_W
    t = jnp.arange(rows * GRID_W)
    row = (t // GRID_W).astype(jnp.float32)
    col = (t % GRID_W).astype(jnp.float32)
    nf = HEAD_DIM // 4
    inv = ROPE_BASE ** (-jnp.arange(nf, dtype=jnp.float32) / nf)
    ang = jnp.concatenate([row[:, None] * inv, col[:, None] * inv], axis=-1)
    return jnp.cos(ang), jnp.sin(ang)


def apply_axial_rope(x, cos, sin):
    B, T, H, hd = x.shape
    nf = hd // 4
    xf = x.astype(jnp.float32).reshape(B, T, H, 2, 2, nf)
    c = cos.reshape(T, 2, nf)[None, :, None]
    s = sin.reshape(T, 2, nf)[None, :, None]
    x1, x2 = xf[..., 0, :], xf[..., 1, :]
    out = jnp.stack([x1 * c - x2 * s, x2 * c + x1 * s], axis=-2)
    return out.reshape(B, T, H, hd).astype(x.dtype)


def gqa_sink_attention(q, k, v, mask, sink):
    B, Q, HQ, hd = q.shape
    HKV = k.shape[2]
    G = HQ // HKV
    qg = q.astype(jnp.float32).reshape(B, Q, HKV, G, hd)
    s = jnp.einsum('bqhgd,bkhd->bhgqk', qg, k.astype(jnp.float32)) * (hd ** -0.5)
    if mask is not None:
        s = jnp.where(mask, s, NEG_INF)
    sink_col = jnp.broadcast_to(sink.astype(jnp.float32).reshape(1, HKV, G, 1, 1), (B, HKV, G, Q, 1))
    p = jax.nn.softmax(jnp.concatenate([s, sink_col], axis=-1), axis=-1)[..., :-1]
    o = jnp.einsum('bhgqk,bkhd->bqhgd', p, v.astype(jnp.float32))
    return o.reshape(B, Q, HQ, hd).astype(q.dtype)


def context_attention(q, k, v, sink):
    B, S, HQ, hd = q.shape
    nb = S // ATTN_BLOCK
    qb = q.reshape(B, nb, ATTN_BLOCK, HQ, hd).swapaxes(0, 1)
    o = lax.map(lambda qi: gqa_sink_attention(qi, k, v, None, sink), qb)
    return o.swapaxes(0, 1).reshape(B, S, HQ, hd)


def latent_attention(q, k, v, k_ctx, v_ctx, sink):
    B, T, HQ, hd = q.shape
    nb = T // ATTN_BLOCK
    span = ATTN_BLOCK + 2 * WINDOW
    pad = ((0, 0), (WINDOW, WINDOW), (0, 0), (0, 0))
    kp = jnp.pad(k, pad)
    vp = jnp.pad(v, pad)
    qb = q.reshape(B, nb, ATTN_BLOCK, HQ, hd).swapaxes(0, 1)
    rel = jnp.arange(span)[None, :] - WINDOW - jnp.arange(ATTN_BLOCK)[:, None]
    ctx_ok = jnp.ones((ATTN_BLOCK, k_ctx.shape[1]), dtype=bool)

    def block(args):
        i, qi = args
        start = i * ATTN_BLOCK
        kb = lax.dynamic_slice_in_dim(kp, start, span, axis=1)
        vb = lax.dynamic_slice_in_dim(vp, start, span, axis=1)
        kpos = start - WINDOW + jnp.arange(span)
        band_ok = (jnp.abs(rel) <= WINDOW) & ((kpos >= 0) & (kpos < T))[None, :]
        mask = jnp.concatenate([ctx_ok, band_ok], axis=1)
        keys = jnp.concatenate([k_ctx, kb], axis=1)
        vals = jnp.concatenate([v_ctx, vb], axis=1)
        return gqa_sink_attention(qi, keys, vals, mask, sink)

    o = lax.map(block, (jnp.arange(nb), qb))
    return o.swapaxes(0, 1).reshape(B, T, HQ, hd)


def ab_mixer(h, w_in, conv_w, a_log, dt_bias, onorm_g, sink, w_out, s0, kv_ctx, rope):
    B, T, _ = h.shape
    cuts = np.cumsum(AB_SPLITS)[:-1].tolist()
    qkv_a, gate_a, alpha_a, beta_a, q_b, k_b, v_b = jnp.split(mm3(h, w_in), cuts, axis=-1)
    o_a, s_fin = delta_mixer(qkv_a, gate_a, alpha_a.reshape(B, T, 2, A_HEADS), beta_a.reshape(B, T, 2, A_HEADS), conv_w, a_log, dt_bias, onorm_g, s0)
    q_b = q_b.reshape(B, T, B_HEADS, HEAD_DIM)
    k_b = k_b.reshape(B, T, B_KV_HEADS, HEAD_DIM)
    v_b = v_b.reshape(B, T, B_KV_HEADS, HEAD_DIM)
    if rope is None:
        o_b = context_attention(q_b, k_b, v_b, sink)
    else:
        cos, sin = rope
        o_b = latent_attention(apply_axial_rope(q_b, cos, sin), apply_axial_rope(k_b, cos, sin), v_b, kv_ctx[0], kv_ctx[1], sink)
    out = mm3(jnp.concatenate([o_a, o_b.reshape(B, T, B_HEADS * HEAD_DIM)], axis=-1), w_out)
    return out, k_b, v_b, s_fin.astype(h.dtype)


def chunk_mlp_mixer(h, w_in, v_norm_g, w_s, b_s, w_out):
    B, T, _ = h.shape
    nc = T // C_CHUNK
    u, v = jnp.split(jax.nn.gelu(mm3(h, w_in)), 2, axis=-1)
    v = rmsnorm(v, v_norm_g).reshape(B, nc, C_CHUNK, C_GROUPS, C_GROUP_W)
    mixed = jnp.einsum('gij,bnjgc->bnigc', w_s, v) + b_s.T[None, None, :, :, None]
    return mm3(u * mixed.reshape(B, T, C_WIDTH), w_out)


ROUTE_TM = 512
EXPERT_TM = 512
EXPERT_TE = 1024
N_EXPERTS = N_KEYS * N_KEYS


def _top16_ranks(s, iota, iota16):
    cur = s
    rank = jnp.full(s.shape, float(PEER_TOPK), jnp.float32)
    sv = jnp.zeros((PEER_TOPK, s.shape[1]), jnp.float32)
    for k in range(PEER_TOPK):
        m = jnp.max(cur, axis=0, keepdims=True)
        idx = jnp.min(jnp.where(cur == m, iota, float(N_KEYS)), axis=0, keepdims=True)
        sel = iota == idx
        rank = jnp.where(sel, float(k), rank)
        cur = jnp.where(sel, -jnp.inf, cur)
        sv = jnp.where(iota16 == float(k), m, sv)
    return rank, sv


def _staircase(sv0, sv1, iota16):
    n = jnp.zeros_like(sv0)
    f = sv0 + sv1[0:1, :]
    top = f[0:1, :]
    z = jnp.zeros_like(top)
    for _ in range(PEER_TOPK):
        fm = jnp.max(f, axis=0, keepdims=True)
        cand = iota16 * float(PEER_TOPK) + n
        ci = jnp.min(jnp.where(f == fm, cand, 1e9), axis=0, keepdims=True)
        win = cand == ci
        z = z + jnp.exp(fm - top)
        n = jnp.where(win, n + 1.0, n)
        nw = jnp.max(jnp.where(win, n, 0.0), axis=0, keepdims=True)
        nxt = jnp.max(jnp.where(iota16 == nw, sv1, -jnp.inf), axis=0, keepdims=True)
        f = jnp.where(win, sv0 + nxt, f)
    return n, z


def _peer_route_kernel(h_ref, wq_ref, keys_ref, ht_ref, r1_ref, p1_ref, c0_ref, q0_ref, q_scr, s_scr):
    tm = h_ref.shape[0]
    hf = h_ref[...]
    ht_ref[...] = hf.T.astype(jnp.bfloat16)
    q_scr[...] = jnp.dot(hf.astype(jnp.bfloat16), wq_ref[...], preferred_element_type=jnp.float32)

    def head_body(h, carry):
        for p in range(2):
            off = pl.multiple_of(h * D_KEY + p * (D_KEY // 2), LANE)
            qhp = q_scr[:, pl.ds(off, D_KEY // 2)]
            s_scr[p] = lax.dot_general(keys_ref[p], qhp, (((1,), (1,)), ((), ())), precision=lax.Precision.HIGHEST,
                                       preferred_element_type=jnp.float32)

        def chunk_body(c, carry2):
            lanes = pl.ds(pl.multiple_of(c * LANE, LANE), LANE)
            iota = lax.broadcasted_iota(jnp.int32, (N_KEYS, LANE), 0).astype(jnp.float32)
            iota16 = lax.broadcasted_iota(jnp.int32, (PEER_TOPK, LANE), 0).astype(jnp.float32)
            s0 = s_scr[0, :, lanes]
            s1 = s_scr[1, :, lanes]
            rank0, sv0 = _top16_ranks(s0, iota, iota16)
            rank1, sv1 = _top16_ranks(s1, iota, iota16)
            n, z = _staircase(sv0, sv1, iota16)
            c0 = jnp.zeros_like(s0)
            for k in range(PEER_TOPK):
                c0 = jnp.where(rank0 == float(k), n[k:k + 1, :], c0)
            r1_ref[h, :, lanes] = rank1
            p1_ref[h, :, lanes] = jnp.exp(s1 - sv1[0:1, :])
            c0_ref[h, :, lanes] = c0
            q0_ref[h, :, lanes] = jnp.exp(s0 - sv0[0:1, :]) / z
            return carry2

        lax.fori_loop(0, tm // LANE, chunk_body, 0)
        return carry

    lax.fori_loop(0, PEER_HEADS, head_body, 0)


def _gelu_tanh(x):
    return 0.5 * x * (1.0 + jnp.tanh(math.sqrt(2.0 / math.pi) * (x + 0.044715 * (x * x * x))))


def _peer_expert_kernel(ht_ref, r1_ref, p1_ref, c0_ref, q0_ref, ed_ref, eut_ref, out_ref, acc_ref, act_ref, wg_ref):
    e = pl.program_id(1)
    tm = ht_ref.shape[1]
    te = ed_ref.shape[0]

    @pl.when(e == 0)
    def _():
        acc_ref[...] = jnp.zeros_like(acc_ref)

    act_ref[...] = jnp.dot(ed_ref[...], ht_ref[...], preferred_element_type=jnp.float32)

    def chunk_body(c, carry):
        lanes = pl.ds(pl.multiple_of(c * LANE, LANE), LANE)
        for ii in range(te // N_KEYS):
            rows = pl.ds(ii * N_KEYS, N_KEYS)
            g = jnp.zeros((N_KEYS, LANE), jnp.float32)
            for h in range(PEER_HEADS):
                c0row = c0_ref[h, pl.ds(ii, 1), lanes]
                q0row = q0_ref[h, pl.ds(ii, 1), lanes]
                g = g + jnp.where(r1_ref[h, :, lanes] < c0row, p1_ref[h, :, lanes], 0.0) * q0row
            a = _gelu_tanh(act_ref[rows, lanes])
            wg_ref[rows, lanes] = (g * a).astype(jnp.bfloat16)
        return carry

    lax.fori_loop(0, tm // LANE, chunk_body, 0)
    acc_ref[...] += jnp.dot(eut_ref[...], wg_ref[...], preferred_element_type=jnp.float32)

    @pl.when(e == pl.num_programs(1) - 1)
    def _():
        out_ref[...] = acc_ref[...].T


def peer_ffn_2d(h, wq_b, keys_b, ed_b, eut_b):
    T, D = h.shape
    nq = PEER_HEADS * D_KEY
    route_shape = jax.ShapeDtypeStruct((PEER_HEADS, N_KEYS, T), jnp.float32)
    tmr = ROUTE_TM
    route_spec = pl.BlockSpec((PEER_HEADS, N_KEYS, tmr), lambda i: (0, 0, i))
    ht, r1, p1, c0, q0 = pl.pallas_call(
        _peer_route_kernel,
        grid=(T // tmr,),
        in_specs=[pl.BlockSpec((tmr, D), lambda i: (i, 0)), pl.BlockSpec((D, nq), lambda i: (0, 0)),
                  pl.BlockSpec((2, N_KEYS, D_KEY // 2), lambda i: (0, 0, 0))],
        out_specs=[pl.BlockSpec((D, tmr), lambda i: (0, i)), route_spec, route_spec, route_spec, route_spec],
        out_shape=[jax.ShapeDtypeStruct((D, T), jnp.bfloat16), route_shape, route_shape, route_shape, route_shape],
        scratch_shapes=[pltpu.VMEM((tmr, nq), jnp.float32), pltpu.VMEM((2, N_KEYS, tmr), jnp.float32)],
        compiler_params=pltpu.CompilerParams(dimension_semantics=("arbitrary",), vmem_limit_bytes=VMEM_LIMIT),
        name="peer_route",
    )(h, wq_b, keys_b)

    tm, te = EXPERT_TM, EXPERT_TE
    full_spec = pl.BlockSpec((PEER_HEADS, N_KEYS, tm), lambda i, e: (0, 0, i))
    row_spec = pl.BlockSpec((PEER_HEADS, te // N_KEYS, tm), lambda i, e: (0, e, i))
    return pl.pallas_call(
        _peer_expert_kernel,
        grid=(T // tm, N_EXPERTS // te),
        in_specs=[pl.BlockSpec((D, tm), lambda i, e: (0, i)), full_spec, full_spec, row_spec, row_spec,
                  pl.BlockSpec((te, D), lambda i, e: (e, 0)), pl.BlockSpec((D, te), lambda i, e: (0, e))],
        out_specs=pl.BlockSpec((tm, D), lambda i, e: (i, 0)),
        out_shape=jax.ShapeDtypeStruct((T, D), jnp.float32),
        scratch_shapes=[pltpu.VMEM((D, tm), jnp.float32), pltpu.VMEM((te, tm), jnp.float32), pltpu.VMEM((te, tm), jnp.bfloat16)],
        compiler_params=pltpu.CompilerParams(dimension_semantics=("arbitrary", "arbitrary"), vmem_limit_bytes=VMEM_LIMIT),
        name="peer_expert",
    )(ht, r1, p1, c0, q0, ed_b, eut_b)


def peer_ffn(h, w_q, sub_keys, e_down, e_up):
    B, T, D = h.shape
    out = peer_ffn_2d(h.reshape(B * T, D), w_q.astype(jnp.bfloat16), sub_keys,
                      e_down.astype(jnp.bfloat16), e_up.T.astype(jnp.bfloat16))
    return out.reshape(B, T, D)


def kernel(x_prompt, x_sample, cache_attn_k, cache_attn_v, state_delta, c, c_ctx, w_mod, b_mod, norm_g, w_in_ab, conv_ab, a_log_ab, dt_bias_ab, onorm_ab, sink_ab, w_out_ab, w_in_c, vnorm_c, w_s_c, b_s_c, w_out_c, peer_wq, peer_keys, peer_down, peer_up, final_norm_g):
    xp = x_prompt
    xs = x_sample
    rope = axial_rope(x_sample.shape[1])
    zero_state = jnp.zeros((x_prompt.shape[0], 2, A_HEADS, A_DK, A_DV), jnp.float32)
    new_k, new_v, new_s = [], [], []
    for layer in range(DEPTH):
        mp = modulation(c_ctx, w_mod[layer], b_mod[layer])
        ms = modulation(c, w_mod[layer], b_mod[layer])
        hp = modulated_norm(xp, norm_g[layer, 0], mp[0], mp[1])
        hs = modulated_norm(xs, norm_g[layer, 0], ms[0], ms[1])
        li = layer // 2
        if layer % 2 == 0:
            wts = (w_in_ab[li], conv_ab[li], a_log_ab[li], dt_bias_ab[li], onorm_ab[li], sink_ab[li], w_out_ab[li])
            op, kc, vc, sc = ab_mixer(hp, *wts, zero_state, None, None)
            os_, _, _, _ = ab_mixer(hs, *wts, state_delta[:, li], (cache_attn_k[:, li], cache_attn_v[:, li]), rope)
            new_k.append(kc)
            new_v.append(vc)
            new_s.append(sc)
        else:
            wts = (w_in_c[li], vnorm_c[li], w_s_c[li], b_s_c[li], w_out_c[li])
            op = chunk_mlp_mixer(hp, *wts)
            os_ = chunk_mlp_mixer(hs, *wts)
        xp = xp + mp[2] * op
        xs = xs + ms[2] * os_
        hp = modulated_norm(xp, norm_g[layer, 1], mp[3], mp[4])
        hs = modulated_norm(xs, norm_g[layer, 1], ms[3], ms[4])
        pw = (peer_wq[layer], peer_keys[layer], peer_down[layer], peer_up[layer])
        xp = xp + mp[5] * peer_ffn(hp, *pw)
        xs = xs + ms[5] * peer_ffn(hs, *pw)
    y_prompt = rmsnorm(xp, final_norm_g)
    y_sample = rmsnorm(xs, final_norm_g)
    new_attn_k = jnp.stack(new_k, axis=1)
    new_attn_v = jnp.stack(new_v, axis=1)
    new_state_delta = jnp.stack(new_s, axis=1)
    return (y_prompt, y_sample, new_attn_k, new_attn_v, new_state_delta)
```

```python
import functools
import math

import numpy as np
import jax
import jax.numpy as jnp
from jax import lax
from jax.experimental import pallas as pl
from jax.experimental.pallas import tpu as pltpu

D_MODEL = 1024
DEPTH = 2
GRID_W = 64
HEAD_DIM = 128
EPS = 1e-6
NEG_INF = -1e30
A_HEADS = D_MODEL // (2 * HEAD_DIM)
A_W = A_HEADS * HEAD_DIM
A_CONV = 5
B_HEADS = D_MODEL // (2 * HEAD_DIM)
B_KV_HEADS = B_HEADS // 2
B_GROUP = B_HEADS // B_KV_HEADS
WINDOW = 128
ATTN_BLOCK = 128
ROPE_BASE = 10000.0
C_WIDTH = D_MODEL
C_GROUPS = 8
C_GROUP_W = C_WIDTH // C_GROUPS
C_CHUNK = 128
PEER_HEADS = 8
N_KEYS = 128
N_EXPERTS = N_KEYS * N_KEYS
PEER_TOPK = 16
D_KEY = 256

LANE = 128
VMEM_LIMIT = 48 * 1024 * 1024
PROJ_TM = 512
ROUTE_TM = 512
EXPERT_TM = 512
EXPERT_TE = 1024
MOD_TN = 1536
DC = 128

QKV_W = 3 * A_W
QB_W = B_HEADS * HEAD_DIM
KV_W = B_KV_HEADS * HEAD_DIM
AB_COLS = 4 * A_HEADS
AB_MAIN = QKV_W + A_W + QB_W + 2 * KV_W


def _bdot(a, b):
    return jnp.dot(a.astype(jnp.bfloat16), b.astype(jnp.bfloat16), preferred_element_type=jnp.float32)


def _bdot_nt(a, b):
    return lax.dot_general(a.astype(jnp.bfloat16), b.astype(jnp.bfloat16), (((1,), (1,)), ((), ())),
                           preferred_element_type=jnp.float32)


def _hdot(a, b):
    return jnp.dot(a, b, precision=lax.Precision.HIGHEST, preferred_element_type=jnp.float32)


def _sigmoid(x):
    return 1.0 / (1.0 + jnp.exp(-x))


def _softplus(x):
    return jnp.maximum(x, 0.0) + jnp.log(1.0 + jnp.exp(-jnp.abs(x)))


def _gelu_tanh(x):
    return 0.5 * x * (1.0 + jnp.tanh(math.sqrt(2.0 / math.pi) * (x + 0.044715 * (x * x * x))))


def _rms(x):
    return x * lax.rsqrt(jnp.mean(x * x, axis=-1, keepdims=True) + EPS)


def _modnorm(x, g, shift, scale):
    return _rms(x) * g * (1.0 + scale) + shift


def _cparams(n):
    return pltpu.CompilerParams(dimension_semantics=("arbitrary",) * n, vmem_limit_bytes=VMEM_LIMIT)


def _mod_spec(tm, tokens_per_mod):
    return pl.BlockSpec((1, 1, D_MODEL), lambda i, *_: ((i * tm) // tokens_per_mod, 0, 0))


def _row_spec(tm, width):
    return pl.BlockSpec((tm, width), lambda i: (i, 0))


def _full_spec(shape):
    return pl.BlockSpec(shape, lambda *_: (0,) * len(shape))


def _mod_kernel(c_ref, w_ref, b_ref, o_ref):
    c = c_ref[...]
    o_ref[...] = _bdot(c * _sigmoid(c), w_ref[...]) + b_ref[...]


def modulation_pallas(cvecs, w_mod, b_mod):
    R, D = cvecs.shape
    N = w_mod.shape[1]
    return pl.pallas_call(
        _mod_kernel, grid=(N // MOD_TN,),
        in_specs=[pl.BlockSpec((R, D), lambda j: (0, 0)), pl.BlockSpec((D, MOD_TN), lambda j: (0, j)),
                  pl.BlockSpec((1, MOD_TN), lambda j: (0, j))],
        out_specs=pl.BlockSpec((R, MOD_TN), lambda j: (0, j)),
        out_shape=jax.ShapeDtypeStruct((R, N), jnp.float32),
        compiler_params=_cparams(1), name="modulation",
    )(cvecs, w_mod, b_mod.reshape(1, N))


def _ab_in_kernel(x_ref, sh_ref, sc_ref, g_ref, w_ref, wl_ref, wlt_ref,
                  qkv_ref, gate_ref, qb_ref, kb_ref, vb_ref, ab_ref, abt_ref):
    hb = _modnorm(x_ref[...], g_ref[...], sh_ref[0], sc_ref[0]).astype(jnp.bfloat16)
    res = jnp.dot(hb, w_ref[...], preferred_element_type=jnp.float32)
    o = 0
    for ref, wd in ((qkv_ref, QKV_W), (gate_ref, A_W), (qb_ref, QB_W), (kb_ref, KV_W), (vb_ref, KV_W)):
        ref[...] = res[:, o:o + wd]
        o += wd
    ab_ref[...] = jnp.dot(hb, wl_ref[...], preferred_element_type=jnp.float32)
    abt_ref[...] = lax.dot_general(wlt_ref[...], hb, (((1,), (1,)), ((), ())), preferred_element_type=jnp.float32)


def ab_in_pallas(x, shift, scale, g, w_main, w_log_pad, w_log_t):
    N, D = x.shape
    tm = PROJ_TM
    ms = _mod_spec(tm, N // shift.shape[0])
    outs = [(N, QKV_W), (N, A_W), (N, QB_W), (N, KV_W), (N, KV_W), (N, LANE)]
    return pl.pallas_call(
        _ab_in_kernel, grid=(N // tm,),
        in_specs=[_row_spec(tm, D), ms, ms, _full_spec((1, D)), _full_spec((D, AB_MAIN)), _full_spec((D, LANE)),
                  _full_spec((AB_COLS, D))],
        out_specs=[_row_spec(tm, s[1]) for s in outs] + [pl.BlockSpec((AB_COLS, tm), lambda i: (0, i))],
        out_shape=[jax.ShapeDtypeStruct(s, jnp.float32) for s in outs] + [jax.ShapeDtypeStruct((AB_COLS, N), jnp.float32)],
        compiler_params=_cparams(1), name="ab_in",
    )(x, shift, scale, g.reshape(1, D), w_main, w_log_pad, w_log_t)


def _ab_out_kernel(oa_ref, ob_ref, w_ref, x_ref, gt_ref, sh_ref, sc_ref, g_ref, xn_ref, hn_ref):
    res = _bdot(oa_ref[...], w_ref[0:A_W, :]) + _bdot(ob_ref[...], w_ref[A_W:, :])
    xn = x_ref[...] + gt_ref[0] * res
    xn_ref[...] = xn
    hn_ref[...] = _modnorm(xn, g_ref[...], sh_ref[0], sc_ref[0])


def ab_out_pallas(oa, ob, w_out_b, x, gate, shift, scale, g):
    N, D = x.shape
    tm = PROJ_TM
    ms = _mod_spec(tm, N // gate.shape[0])
    return pl.pallas_call(
        _ab_out_kernel, grid=(N // tm,),
        in_specs=[_row_spec(tm, A_W), _row_spec(tm, QB_W), _full_spec((A_W + QB_W, D)), _row_spec(tm, D), ms, ms, ms,
                  _full_spec((1, D))],
        out_specs=[_row_spec(tm, D), _row_spec(tm, D)],
        out_shape=[jax.ShapeDtypeStruct((N, D), jnp.float32)] * 2,
        compiler_params=_cparams(1), name="ab_out",
    )(oa, ob, w_out_b, x, gate, shift, scale, g.reshape(1, D))


def _c_in_kernel(x_ref, sh_ref, sc_ref, g_ref, w_ref, vg_ref, u_ref, vn_ref):
    hb = _modnorm(x_ref[...], g_ref[...], sh_ref[0], sc_ref[0]).astype(jnp.bfloat16)
    res = _gelu_tanh(jnp.dot(hb, w_ref[...], preferred_element_type=jnp.float32))
    u_ref[...] = res[:, :C_WIDTH]
    vn_ref[...] = _rms(res[:, C_WIDTH:]) * vg_ref[...]


def c_in_pallas(x, shift, scale, g, w_in_b, vnorm_g):
    N, D = x.shape
    tm = PROJ_TM
    ms = _mod_spec(tm, N // shift.shape[0])
    return pl.pallas_call(
        _c_in_kernel, grid=(N // tm,),
        in_specs=[_row_spec(tm, D), ms, ms, _full_spec((1, D)), _full_spec((D, 2 * C_WIDTH)), _full_spec((1, C_WIDTH))],
        out_specs=[_row_spec(tm, C_WIDTH), _row_spec(tm, C_WIDTH)],
        out_shape=[jax.ShapeDtypeStruct((N, C_WIDTH), jnp.float32)] * 2,
        compiler_params=_cparams(1), name="c_in",
    )(x, shift, scale, g.reshape(1, D), w_in_b, vnorm_g.reshape(1, C_WIDTH))


def _c_mix_kernel(u_ref, vn_ref, ws_ref, bs_ref, w_ref, x_ref, gt_ref, sh_ref, sc_ref, g_ref, xn_ref, hn_ref, z_ref):
    tm = u_ref.shape[0]
    for cc in range(tm // C_CHUNK):
        rows = slice(cc * C_CHUNK, (cc + 1) * C_CHUNK)
        for grp in range(C_GROUPS):
            cols = slice(grp * C_GROUP_W, (grp + 1) * C_GROUP_W)
            mixed = _bdot(ws_ref[grp], vn_ref[rows, cols]) + bs_ref[:, grp:grp + 1]
            z_ref[rows, cols] = (u_ref[rows, cols] * mixed).astype(jnp.bfloat16)
    res = jnp.dot(z_ref[...], w_ref[...], preferred_element_type=jnp.float32)
    xn = x_ref[...] + gt_ref[0] * res
    xn_ref[...] = xn
    hn_ref[...] = _modnorm(xn, g_ref[...], sh_ref[0], sc_ref[0])


def c_mix_pallas(u, vn, w_s_b, bs, w_out_b, x, gate, shift, scale, g):
    N, D = x.shape
    tm = PROJ_TM
    ms = _mod_spec(tm, N // gate.shape[0])
    return pl.pallas_call(
        _c_mix_kernel, grid=(N // tm,),
        in_specs=[_row_spec(tm, C_WIDTH), _row_spec(tm, C_WIDTH), _full_spec((C_GROUPS, C_CHUNK, C_CHUNK)),
                  _full_spec((C_CHUNK, LANE)), _full_spec((C_WIDTH, D)), _row_spec(tm, D), ms, ms, ms, _full_spec((1, D))],
        out_specs=[_row_spec(tm, D), _row_spec(tm, D)],
        out_shape=[jax.ShapeDtypeStruct((N, D), jnp.float32)] * 2,
        scratch_shapes=[pltpu.VMEM((tm, C_WIDTH), jnp.bfloat16)],
        compiler_params=_cparams(1), name="c_mix",
    )(u, vn, w_s_b, bs, w_out_b, x, gate, shift, scale, g.reshape(1, D))


def _unit_tri_inverse_offdiag(a, ii, jj):
    def same_block(b):
        return jnp.right_shift(ii, b) == jnp.right_shift(jj, b)
    n = -jnp.where(same_block(1), a, 0.0)
    for b in range(1, 7):
        o = jnp.where(same_block(b + 1) & jnp.logical_not(same_block(b)), a, 0.0)
        p = o + _bdot(n, o)
        q = p + _bdot(p, n)
        n = n - q
    return n


def _delta_kernel(q_ref, k_ref, v_ref, gate_ref, ab_ref, abt_ref, cw_ref, prow_ref, pcol_ref, og_ref, s0_ref,
                  o_ref, sfin_ref, q_s, k_s, v_s, kt_s, gc_s, bc_s, gr_s, of_s, ob_s, *, use_s0):
    T = q_ref.shape[1]
    nck = T // DC
    h = pl.program_id(1)
    ii = lax.broadcasted_iota(jnp.int32, (DC, DC), 0)
    jj = lax.broadcasted_iota(jnp.int32, (DC, DC), 1)
    low = (jj <= ii).astype(jnp.float32)
    upp = (jj >= ii).astype(jnp.float32)
    r16 = lax.broadcasted_iota(jnp.int32, (AB_COLS, DC), 0)

    def shifted(ref, c, d):
        cur = ref[0, pl.ds(pl.multiple_of(c * DC, DC), DC), :]
        if d == 0:
            return cur
        step = 1 if d > 0 else -1
        cn = jnp.clip(c + step, 0, nck - 1)
        nb = ref[0, pl.ds(pl.multiple_of(cn * DC, DC), DC), :]
        ok = jnp.where((c + step >= 0) & (c + step < nck), 1.0, 0.0)
        rc = pltpu.roll(cur, (-d) % DC, axis=0)
        rn = pltpu.roll(nb, (-d) % DC, axis=0) * ok
        return jnp.where((ii + d >= 0) & (ii + d < DC), rc, rn)

    def prep(c, carry):
        rows = pl.ds(pl.multiple_of(c * DC, DC), DC)

        def conv_silu(ref, idx):
            acc = jnp.zeros((DC, HEAD_DIM), jnp.float32)
            for kk in range(A_CONV):
                acc = acc + shifted(ref, c, kk - A_CONV // 2) * cw_ref[0, idx, kk:kk + 1, :]
            return acc * _sigmoid(acc)

        q = conv_silu(q_ref, 0)
        k = conv_silu(k_ref, 1)
        v = conv_silu(v_ref, 2)
        q = q * lax.rsqrt(jnp.sum(q * q, axis=-1, keepdims=True) + EPS) * (HEAD_DIM ** -0.5)
        k = k * lax.rsqrt(jnp.sum(k * k, axis=-1, keepdims=True) + EPS)
        q_s[rows, :] = q
        k_s[rows, :] = k
        v_s[rows, :] = v
        kt_s[c] = k.T
        ab = ab_ref[0, rows, :]
        g_all = -prow_ref[0:1, :] * _softplus(ab + prow_ref[1:2, :])
        beta_all = _sigmoid(ab)
        gcf_all = _hdot(low, g_all)
        gcb_all = _hdot(upp, g_all)

        def pick_col(x, col):
            return jnp.sum(jnp.where(jj == col, x, 0.0), axis=1, keepdims=True)

        gc_s[0, rows, :] = jnp.broadcast_to(pick_col(gcf_all, h), (DC, DC))
        gc_s[1, rows, :] = jnp.broadcast_to(pick_col(gcb_all, A_HEADS + h), (DC, DC))
        bc_s[0, rows, :] = jnp.broadcast_to(pick_col(beta_all, 2 * A_HEADS + h), (DC, DC))
        bc_s[1, rows, :] = jnp.broadcast_to(pick_col(beta_all, 3 * A_HEADS + h), (DC, DC))
        abt = abt_ref[:, rows]
        gr_all = -pcol_ref[0] * _softplus(abt + pcol_ref[1])
        grf_all = _hdot(gr_all, upp)
        grb_all = _hdot(gr_all, low)
        gr_s[c, 0:1, :] = jnp.sum(jnp.where(r16 == h, grf_all, 0.0), axis=0, keepdims=True)
        gr_s[c, 1:2, :] = jnp.sum(jnp.where(r16 == A_HEADS + h, grb_all, 0.0), axis=0, keepdims=True)
        return carry

    lax.fori_loop(0, nck, prep, 0)

    def chunk_update(S, c, dirn):
        rows = pl.ds(pl.multiple_of(c * DC, DC), DC)
        q = q_s[rows, :]
        k = k_s[rows, :]
        v = v_s[rows, :]
        kt = kt_s[c]
        gc = gc_s[dirn, rows, :]
        beta = bc_s[dirn, rows, :]
        gr = gr_s[c, dirn:dirn + 1, :]
        tri = (jj <= ii) if dirn == 0 else (jj >= ii)
        strict = (jj < ii) if dirn == 0 else (jj > ii)
        gamma = jnp.where(tri, jnp.exp(jnp.where(tri, gc - gr, 0.0)), 0.0)
        kb = k * beta
        a = jnp.where(strict, _bdot(kb, kt) * gamma, 0.0)
        egc = jnp.exp(gc)
        rhs_u = v * beta
        rhs_w = kb * egc
        n = _unit_tri_inverse_offdiag(a, ii, jj)
        u = rhs_u + _bdot(n, rhs_u)
        w = rhs_w + _bdot(n, rhs_w)
        qk = _bdot(q, kt) * gamma
        gl = gc[DC - 1:DC, :] if dirn == 0 else gc[0:1, :]
        ktt = kt * jnp.exp(gl - gr)
        v_new = u - _bdot(w, S)
        o = _bdot(q * egc, S) + _bdot(qk, v_new)
        S = S * jnp.exp(gl) + _bdot(ktt, v_new)
        return S, o, rows

    def step(i, carry):
        sf, sb = carry
        sf, o_f, rows_f = chunk_update(sf, i, 0)
        of_s[rows_f, :] = o_f
        sb, o_b, rows_b = chunk_update(sb, nck - 1 - i, 1)
        ob_s[rows_b, :] = o_b
        return sf, sb

    if use_s0:
        init = (s0_ref[0, 0, 0], s0_ref[0, 1, 0])
    else:
        init = (jnp.zeros((HEAD_DIM, HEAD_DIM), jnp.float32), jnp.zeros((HEAD_DIM, HEAD_DIM), jnp.float32))
    sf, sb = lax.fori_loop(0, nck, step, init)
    sfin_ref[0, 0, 0] = sf
    sfin_ref[0, 1, 0] = sb

    def fin(c, carry):
        rows = pl.ds(pl.multiple_of(c * DC, DC), DC)
        o = of_s[rows, :] + ob_s[rows, :]
        o = _rms(o) * og_ref[...]
        g = gate_ref[0, rows, :]
        o_ref[0, rows, :] = o * (g * _sigmoid(g))
        return carry

    lax.fori_loop(0, nck, fin, 0)


def delta_mixer_pallas(qkv, gate, ab, abt, cw, prow, pcol, onorm_g, s0):
    B, T, _ = qkv.shape
    H = A_HEADS
    state_block = (1, 2, 1, HEAD_DIM, HEAD_DIM)
    use_s0 = s0 is not None
    if use_s0:
        s0_spec = pl.BlockSpec(state_block, lambda b, h: (b, 0, h, 0, 0))
    else:
        s0 = jnp.zeros(state_block, jnp.float32)
        s0_spec = _full_spec(state_block)
    nck = T // DC
    seq = lambda off: pl.BlockSpec((1, T, HEAD_DIM), lambda b, h: (b, 0, off + h))
    tok = pltpu.VMEM((T, HEAD_DIM), jnp.float32)
    return pl.pallas_call(
        functools.partial(_delta_kernel, use_s0=use_s0),
        grid=(B, H),
        in_specs=[seq(0), seq(H), seq(2 * H), seq(0),
                  pl.BlockSpec((1, T, LANE), lambda b, h: (b, 0, 0)), pl.BlockSpec((AB_COLS, T), lambda b, h: (0, b)),
                  pl.BlockSpec((1, 3, A_CONV, HEAD_DIM), lambda b, h: (h, 0, 0, 0)),
                  _full_spec((2, LANE)), _full_spec((2, AB_COLS, LANE)), _full_spec((1, HEAD_DIM)), s0_spec],
        out_specs=[seq(0), pl.BlockSpec(state_block, lambda b, h: (b, 0, h, 0, 0))],
        out_shape=[jax.ShapeDtypeStruct((B, T, A_W), jnp.float32),
                   jax.ShapeDtypeStruct((B, 2, H, HEAD_DIM, HEAD_DIM), jnp.float32)],
        scratch_shapes=[tok, tok, tok, pltpu.VMEM((nck, HEAD_DIM, DC), jnp.float32),
                        pltpu.VMEM((2, T, DC), jnp.float32), pltpu.VMEM((2, T, DC), jnp.float32),
                        pltpu.VMEM((nck, 8, DC), jnp.float32), tok, tok],
        compiler_params=_cparams(2), name="delta",
    )(qkv, qkv, qkv, gate, ab, abt, cw, prow, pcol, onorm_g, s0)


def _softmax_segments(scores, values, sink):
    m = jnp.full((scores[0].shape[0], 1), sink, jnp.float32)
    for s in scores:
        m = jnp.maximum(m, jnp.max(s, axis=-1, keepdims=True))
    den = jnp.exp(sink - m)
    o = None
    for s, v in zip(scores, values):
        p = jnp.exp(s - m)
        den = den + jnp.sum(p, axis=-1, keepdims=True)
        pv = _bdot(p, v)
        o = pv if o is None else o + pv
    return o / den


def _ctx_attn_kernel(sink_ref, q_ref, k_ref, v_ref, o_ref):
    hkv = pl.program_id(1)
    k = k_ref[0]
    v = v_ref[0]
    for g in range(B_GROUP):
        q = q_ref[0, :, g * HEAD_DIM:(g + 1) * HEAD_DIM]
        s = _bdot_nt(q, k) * (HEAD_DIM ** -0.5)
        o_ref[0, :, g * HEAD_DIM:(g + 1) * HEAD_DIM] = _softmax_segments([s], [v], sink_ref[hkv * B_GROUP + g])


def context_attention_pallas(qb, kb, vb, sink):
    B, S, _ = qb.shape
    gw = B_GROUP * HEAD_DIM
    return pl.pallas_call(
        _ctx_attn_kernel,
        grid=(B, B_KV_HEADS),
        in_specs=[pl.BlockSpec(memory_space=pltpu.SMEM),
                  pl.BlockSpec((1, S, gw), lambda b, h: (b, 0, h)),
                  pl.BlockSpec((1, S, HEAD_DIM), lambda b, h: (b, 0, h)),
                  pl.BlockSpec((1, S, HEAD_DIM), lambda b, h: (b, 0, h))],
        out_specs=pl.BlockSpec((1, S, gw), lambda b, h: (b, 0, h)),
        out_shape=jax.ShapeDtypeStruct(qb.shape, jnp.float32),
        compiler_params=_cparams(2), name="ctx_attn",
    )(sink, qb, kb, vb)


def rope_tables(n_tokens):
    t = np.arange((n_tokens // GRID_W) * GRID_W)
    row = jnp.asarray((t // GRID_W).astype(np.float32))
    col = jnp.asarray((t % GRID_W).astype(np.float32))
    nf = HEAD_DIM // 4
    inv = ROPE_BASE ** (-jnp.arange(nf, dtype=jnp.float32) / nf)
    ar = row[:, None] * inv
    ac = col[:, None] * inv
    rc = jnp.concatenate([jnp.cos(ar), jnp.cos(ar), jnp.cos(ac), jnp.cos(ac)], axis=-1)
    rs = jnp.concatenate([-jnp.sin(ar), jnp.sin(ar), -jnp.sin(ac), jnp.sin(ac)], axis=-1)
    return rc, rs


def _rope(x, rc, rs):
    lane = lax.broadcasted_iota(jnp.int32, x.shape, 1)
    nf = HEAD_DIM // 4
    swapped = jnp.where((lane & (2 * nf - 1)) < nf, pltpu.roll(x, HEAD_DIM - nf, axis=1), pltpu.roll(x, nf, axis=1))
    return x * rc + swapped * rs


def _lat_attn_kernel(sink_ref, q_ref, kp_ref, kc_ref, kn_ref, vp_ref, vc_ref, vn_ref, kx_ref, vx_ref,
                     rcp_ref, rcc_ref, rcn_ref, rsp_ref, rsc_ref, rsn_ref, o_ref):
    hkv = pl.program_id(1)
    i = pl.program_id(2)
    nb = pl.num_programs(2)
    r = lax.broadcasted_iota(jnp.int32, (ATTN_BLOCK, ATTN_BLOCK), 0)
    c = lax.broadcasted_iota(jnp.int32, (ATTN_BLOCK, ATTN_BLOCK), 1)
    kp = _rope(kp_ref[0], rcp_ref[...], rsp_ref[...])
    kc = _rope(kc_ref[0], rcc_ref[...], rsc_ref[...])
    kn = _rope(kn_ref[0], rcn_ref[...], rsn_ref[...])
    ok_p = (c >= r) & (i > 0)
    ok_n = (c <= r) & (i < nb - 1)
    scale = HEAD_DIM ** -0.5
    for g in range(B_GROUP):
        q = _rope(q_ref[0, :, g * HEAD_DIM:(g + 1) * HEAD_DIM], rcc_ref[...], rsc_ref[...])
        sx = _bdot_nt(q, kx_ref[0]) * scale
        sp = jnp.where(ok_p, _bdot_nt(q, kp) * scale, NEG_INF)
        sc = _bdot_nt(q, kc) * scale
        sn = jnp.where(ok_n, _bdot_nt(q, kn) * scale, NEG_INF)
        o_ref[0, :, g * HEAD_DIM:(g + 1) * HEAD_DIM] = _softmax_segments(
            [sx, sp, sc, sn], [vx_ref[0], vp_ref[0], vc_ref[0], vn_ref[0]], sink_ref[hkv * B_GROUP + g])


def latent_attention_pallas(qb, kb, vb, k_ctx, v_ctx, sink):
    assert WINDOW == ATTN_BLOCK
    B, T, _ = qb.shape
    P = k_ctx.shape[1]
    nb = T // ATTN_BLOCK
    gw = B_GROUP * HEAD_DIM
    rc, rs = rope_tables(T)
    prev = lambda b, h, i: (b, jnp.maximum(i - 1, 0), h)
    cur = lambda b, h, i: (b, i, h)
    nxt = lambda b, h, i: (b, jnp.minimum(i + 1, nb - 1), h)
    blk = lambda f: pl.BlockSpec((1, ATTN_BLOCK, HEAD_DIM), f)
    tab = lambda f: pl.BlockSpec((ATTN_BLOCK, HEAD_DIM), lambda b, h, i: (f(b, h, i)[1], 0))
    ctx = pl.BlockSpec((1, P, HEAD_DIM), lambda b, h, i: (b, 0, h))
    return pl.pallas_call(
        _lat_attn_kernel,
        grid=(B, B_KV_HEADS, nb),
        in_specs=[pl.BlockSpec(memory_space=pltpu.SMEM), pl.BlockSpec((1, ATTN_BLOCK, gw), cur),
                  blk(prev), blk(cur), blk(nxt), blk(prev), blk(cur), blk(nxt), ctx, ctx,
                  tab(prev), tab(cur), tab(nxt), tab(prev), tab(cur), tab(nxt)],
        out_specs=pl.BlockSpec((1, ATTN_BLOCK, gw), cur),
        out_shape=jax.ShapeDtypeStruct(qb.shape, jnp.float32),
        compiler_params=_cparams(3), name="lat_attn",
    )(sink, qb, kb, kb, kb, vb, vb, vb, k_ctx, v_ctx, rc, rc, rc, rs, rs, rs)


def _top16_ranks(s, iota, iota16):
    cur = s
    rank = jnp.full(s.shape, float(PEER_TOPK), jnp.float32)
    sv = jnp.zeros((PEER_TOPK, s.shape[1]), jnp.float32)
    for k in range(PEER_TOPK):
        m = jnp.max(cur, axis=0, keepdims=True)
        idx = jnp.min(jnp.where(cur == m, iota, float(N_KEYS)), axis=0, keepdims=True)
        sel = iota == idx
        rank = jnp.where(sel, float(k), rank)
        cur = jnp.where(sel, -jnp.inf, cur)
        sv = jnp.where(iota16 == float(k), m, sv)
    return rank, sv


def _staircase(sv0, sv1, iota16):
    n = jnp.zeros_like(sv0)
    f = sv0 + sv1[0:1, :]
    top = f[0:1, :]
    z = jnp.zeros_like(top)
    for _ in range(PEER_TOPK):
        fm = jnp.max(f, axis=0, keepdims=True)
        cand = iota16 * float(PEER_TOPK) + n
        ci = jnp.min(jnp.where(f == fm, cand, 1e9), axis=0, keepdims=True)
        win = cand == ci
        z = z + jnp.exp(fm - top)
        n = jnp.where(win, n + 1.0, n)
        nw = jnp.max(jnp.where(win, n, 0.0), axis=0, keepdims=True)
        nxt = jnp.max(jnp.where(iota16 == nw, sv1, -jnp.inf), axis=0, keepdims=True)
        f = jnp.where(win, sv0 + nxt, f)
    return n, z


def _peer_route_kernel(h_ref, wq_ref, keys_ref, ht_ref, r1_ref, p1_ref, c0_ref, q0_ref, q_scr, s_scr):
    tm = h_ref.shape[0]
    hf = h_ref[...]
    ht_ref[...] = hf.T.astype(jnp.bfloat16)
    q_scr[...] = jnp.dot(hf.astype(jnp.bfloat16), wq_ref[...], preferred_element_type=jnp.float32)

    def head_body(h, carry):
        for p in range(2):
            off = pl.multiple_of(h * D_KEY + p * (D_KEY // 2), LANE)
            qhp = q_scr[:, pl.ds(off, D_KEY // 2)]
            s_scr[p] = lax.dot_general(keys_ref[p], qhp, (((1,), (1,)), ((), ())), precision=lax.Precision.HIGHEST,
                                       preferred_element_type=jnp.float32)

        def chunk_body(c, carry2):
            lanes = pl.ds(pl.multiple_of(c * LANE, LANE), LANE)
            iota = lax.broadcasted_iota(jnp.int32, (N_KEYS, LANE), 0).astype(jnp.float32)
            iota16 = lax.broadcasted_iota(jnp.int32, (PEER_TOPK, LANE), 0).astype(jnp.float32)
            s0 = s_scr[0, :, lanes]
            s1 = s_scr[1, :, lanes]
            rank0, sv0 = _top16_ranks(s0, iota, iota16)
            rank1, sv1 = _top16_ranks(s1, iota, iota16)
            n, z = _staircase(sv0, sv1, iota16)
            c0 = jnp.zeros_like(s0)
            for k in range(PEER_TOPK):
                c0 = jnp.where(rank0 == float(k), n[k:k + 1, :], c0)
            r1_ref[h, :, lanes] = rank1
            p1_ref[h, :, lanes] = jnp.exp(s1 - sv1[0:1, :])
            c0_ref[h, :, lanes] = c0
            q0_ref[h, :, lanes] = jnp.exp(s0 - sv0[0:1, :]) / z
            return carry2

        lax.fori_loop(0, tm // LANE, chunk_body, 0)
        return carry

    lax.fori_loop(0, PEER_HEADS, head_body, 0)


def _peer_expert_kernel(ht_ref, r1_ref, p1_ref, c0_ref, q0_ref, ed_ref, eut_ref, x_ref, gt_ref, fg_ref, out_ref,
                        acc_ref, act_ref, wg_ref, *, final_norm):
    e = pl.program_id(1)
    tm = ht_ref.shape[1]
    te = ed_ref.shape[0]

    @pl.when(e == 0)
    def _():
        acc_ref[...] = jnp.zeros_like(acc_ref)

    act_ref[...] = jnp.dot(ed_ref[...], ht_ref[...], preferred_element_type=jnp.float32)

    def chunk_body(c, carry):
        lanes = pl.ds(pl.multiple_of(c * LANE, LANE), LANE)
        for ii in range(te // N_KEYS):
            rows = pl.ds(ii * N_KEYS, N_KEYS)
            g = jnp.zeros((N_KEYS, LANE), jnp.float32)
            for h in range(PEER_HEADS):
                c0row = c0_ref[h, pl.ds(ii, 1), lanes]
                q0row = q0_ref[h, pl.ds(ii, 1), lanes]
                g = g + jnp.where(r1_ref[h, :, lanes] < c0row, p1_ref[h, :, lanes], 0.0) * q0row
            a = _gelu_tanh(act_ref[rows, lanes])
            wg_ref[rows, lanes] = (g * a).astype(jnp.bfloat16)
        return carry

    lax.fori_loop(0, tm // LANE, chunk_body, 0)
    acc_ref[...] += jnp.dot(eut_ref[...], wg_ref[...], preferred_element_type=jnp.float32)

    @pl.when(e == pl.num_programs(1) - 1)
    def _():
        y = x_ref[...] + gt_ref[0] * acc_ref[...].T
        if final_norm:
            y = _rms(y) * fg_ref[...]
        out_ref[...] = y


def peer_pallas(h, x, gate, wq_b, keys, ed_b, eut_b, final_g, final_norm):
    T, D = h.shape
    nq = PEER_HEADS * D_KEY
    route_shape = jax.ShapeDtypeStruct((PEER_HEADS, N_KEYS, T), jnp.float32)
    tmr = ROUTE_TM
    route_spec = pl.BlockSpec((PEER_HEADS, N_KEYS, tmr), lambda i: (0, 0, i))
    ht, r1, p1, c0, q0 = pl.pallas_call(
        _peer_route_kernel,
        grid=(T // tmr,),
        in_specs=[_row_spec(tmr, D), _full_spec((D, nq)), _full_spec((2, N_KEYS, D_KEY // 2))],
        out_specs=[pl.BlockSpec((D, tmr), lambda i: (0, i)), route_spec, route_spec, route_spec, route_spec],
        out_shape=[jax.ShapeDtypeStruct((D, T), jnp.bfloat16), route_shape, route_shape, route_shape, route_shape],
        scratch_shapes=[pltpu.VMEM((tmr, nq), jnp.float32), pltpu.VMEM((2, N_KEYS, tmr), jnp.float32)],
        compiler_params=_cparams(1), name="peer_route",
    )(h, wq_b, keys)

    tm, te = EXPERT_TM, EXPERT_TE
    full_spec = pl.BlockSpec((PEER_HEADS, N_KEYS, tm), lambda i, e: (0, 0, i))
    row_spec = pl.BlockSpec((PEER_HEADS, te // N_KEYS, tm), lambda i, e: (0, e, i))
    return pl.pallas_call(
        functools.partial(_peer_expert_kernel, final_norm=final_norm),
        grid=(T // tm, N_EXPERTS // te),
        in_specs=[pl.BlockSpec((D, tm), lambda i, e: (0, i)), full_spec, full_spec, row_spec, row_spec,
                  pl.BlockSpec((te, D), lambda i, e: (e, 0)), pl.BlockSpec((D, te), lambda i, e: (0, e)),
                  pl.BlockSpec((tm, D), lambda i, e: (i, 0)), _mod_spec(tm, T // gate.shape[0]), _full_spec((1, D))],
        out_specs=pl.BlockSpec((tm, D), lambda i, e: (i, 0)),
        out_shape=jax.ShapeDtypeStruct((T, D), jnp.float32),
        scratch_shapes=[pltpu.VMEM((D, tm), jnp.float32), pltpu.VMEM((te, tm), jnp.float32), pltpu.VMEM((te, tm), jnp.bfloat16)],
        compiler_params=_cparams(2), name="peer_expert",
    )(ht, r1, p1, c0, q0, ed_b, eut_b, x, gate, final_g.reshape(1, D))


def kernel(x_prompt, x_sample, cache_attn_k, cache_attn_v, state_delta, c, c_ctx, w_mod, b_mod, norm_g, w_in_ab, conv_ab, a_log_ab, dt_bias_ab, onorm_ab, sink_ab, w_out_ab, w_in_c, vnorm_c, w_s_c, b_s_c, w_out_c, peer_wq, peer_keys, peer_down, peer_up, final_norm_g):
    f32, bf16 = jnp.float32, jnp.bfloat16
    BP, SP, D = x_prompt.shape
    BS, TS, _ = x_sample.shape
    past = cache_attn_k.shape[2]
    H = A_HEADS
    cvecs = jnp.zeros((8, D), f32).at[0].set(c_ctx).at[1:1 + BS].set(c)
    streams = [dict(x=x_prompt.reshape(BP * SP, D), B=BP, T=SP, rows=slice(0, 1)),
               dict(x=x_sample.reshape(BS * TS, D), B=BS, T=TS, rows=slice(1, 1 + BS))]
    new_k = new_v = new_s = None
    for layer in range(DEPTH):
        m = modulation_pallas(cvecs, w_mod[layer], b_mod[layer])
        li = layer // 2
        last = layer == DEPTH - 1
        wq_b = peer_wq[layer].astype(bf16)
        ed_b = peer_down[layer].astype(bf16)
        eut_b = peer_up[layer].T.astype(bf16)
        if layer % 2 == 0:
            w_in = w_in_ab[li]
            c0 = QKV_W + A_W
            w_main = jnp.concatenate([w_in[:, :c0], w_in[:, c0 + AB_COLS:]], axis=1).astype(bf16)
            w_log = w_in[:, c0:c0 + AB_COLS].astype(bf16)
            w_log_pad = jnp.pad(w_log, ((0, 0), (0, LANE - AB_COLS)))
            w_out_b = w_out_ab[li].astype(bf16)
            cw = conv_ab[li].reshape(A_CONV, 3, H, HEAD_DIM).transpose(2, 1, 0, 3)
            a_dec = jnp.exp(a_log_ab[li].astype(f32)).reshape(2 * H)
            dt = dt_bias_ab[li].astype(f32).reshape(2 * H)
            prow = jnp.zeros((2, LANE), f32).at[0, :2 * H].set(a_dec).at[1, :2 * H].set(dt)
            pcol = jnp.zeros((2, AB_COLS, LANE), f32).at[0, :2 * H].set(a_dec[:, None]).at[1, :2 * H].set(dt[:, None])
            og = onorm_ab[li].reshape(1, HEAD_DIM).astype(f32)
            sink = sink_ab[li].astype(f32)
        else:
            w_in_b = w_in_c[li].astype(bf16)
            w_s_b = w_s_c[li].astype(bf16)
            bs = jnp.pad(b_s_c[li].T.astype(f32), ((0, 0), (0, LANE - C_GROUPS)))
            w_out_b = w_out_c[li].astype(bf16)
        for si, st in enumerate(streams):
            B, T = st["B"], st["T"]
            mk = [m[st["rows"], k * D:(k + 1) * D].reshape(-1, 1, D) for k in range(6)]
            x = st["x"]
            if layer % 2 == 0:
                qkv, gate, qb, kb, vb, ab, abt = ab_in_pallas(x, mk[0], mk[1], norm_g[layer, 0], w_main, w_log_pad, w_log.T)
                s0 = None if si == 0 else state_delta[:, li]
                o_a, s_fin = delta_mixer_pallas(qkv.reshape(B, T, QKV_W), gate.reshape(B, T, A_W), ab.reshape(B, T, LANE),
                                                abt, cw, prow, pcol, og, s0)
                if si == 0:
                    o_b = context_attention_pallas(qb.reshape(B, T, QB_W), kb.reshape(B, T, KV_W), vb.reshape(B, T, KV_W), sink)
                    new_k = kb.reshape(B, 1, T, B_KV_HEADS, HEAD_DIM)
                    new_v = vb.reshape(B, 1, T, B_KV_HEADS, HEAD_DIM)
                    new_s = s_fin[:, None]
                else:
                    o_b = latent_attention_pallas(qb.reshape(B, T, QB_W), kb.reshape(B, T, KV_W), vb.reshape(B, T, KV_W),
                                                  cache_attn_k[:, li].reshape(B, past, KV_W),
                                                  cache_attn_v[:, li].reshape(B, past, KV_W), sink)
                x, hn = ab_out_pallas(o_a.reshape(B * T, A_W), o_b.reshape(B * T, QB_W), w_out_b, x, mk[2], mk[3], mk[4],
                                      norm_g[layer, 1])
            else:
                u, vn = c_in_pallas(x, mk[0], mk[1], norm_g[layer, 0], w_in_b, vnorm_c[li])
                x, hn = c_mix_pallas(u, vn, w_s_b, bs, w_out_b, x, mk[2], mk[3], mk[4], norm_g[layer, 1])
            st["x"] = peer_pallas(hn, x, mk[5], wq_b, peer_keys[layer], ed_b, eut_b, final_norm_g, last)
    y_prompt = streams[0]["x"].reshape(BP, SP, D)
    y_sample = streams[1]["x"].reshape(BS, TS, D)
    return (y_prompt, y_sample, new_k, new_v, new_s)
```

```python
import functools
import math

import numpy as np
import jax
import jax.numpy as jnp
from jax import lax
from jax.experimental import pallas as pl
from jax.experimental.pallas import tpu as pltpu

D_MODEL = 1024
DEPTH = 2
GRID_W = 64
HEAD_DIM = 128
EPS = 1e-6
NEG_INF = -1e30
A_HEADS = D_MODEL // (2 * HEAD_DIM)
A_W = A_HEADS * HEAD_DIM
A_CONV = 5
B_HEADS = D_MODEL // (2 * HEAD_DIM)
B_KV_HEADS = B_HEADS // 2
B_GROUP = B_HEADS // B_KV_HEADS
WINDOW = 128
ATTN_BLOCK = 128
ROPE_BASE = 10000.0
C_WIDTH = D_MODEL
C_GROUPS = 8
C_GROUP_W = C_WIDTH // C_GROUPS
C_CHUNK = 128
PEER_HEADS = 8
N_KEYS = 128
N_EXPERTS = N_KEYS * N_KEYS
PEER_TOPK = 16
D_KEY = 256

LANE = 128
VMEM_LIMIT = 48 * 1024 * 1024
PROJ_TM = 512
ROUTE_TM = 512
EXPERT_TM = 512
EXPERT_TE = 1024
EXPERT_SUB = 256
MOD_TN = 1536
DC = 128

QKV_W = 3 * A_W
QB_W = B_HEADS * HEAD_DIM
KV_W = B_KV_HEADS * HEAD_DIM
AB_COLS = 4 * A_HEADS
AB_MAIN = QKV_W + A_W + QB_W + 2 * KV_W


def _bdot(a, b):
    return jnp.dot(a.astype(jnp.bfloat16), b.astype(jnp.bfloat16), preferred_element_type=jnp.float32)


def _bdot_nt(a, b):
    return lax.dot_general(a.astype(jnp.bfloat16), b.astype(jnp.bfloat16), (((1,), (1,)), ((), ())),
                           preferred_element_type=jnp.float32)


def _hdot(a, b):
    return jnp.dot(a, b, precision=lax.Precision.HIGHEST, preferred_element_type=jnp.float32)


def _sigmoid(x):
    return 1.0 / (1.0 + jnp.exp(-x))


def _softplus(x):
    return jnp.maximum(x, 0.0) + jnp.log(1.0 + jnp.exp(-jnp.abs(x)))


def _gelu_tanh(x):
    k0 = -2.0 * math.sqrt(2.0 / math.pi) * math.log2(math.e)
    return x / (1.0 + jnp.exp2(x * (k0 + (k0 * 0.044715) * (x * x))))


def _rms(x):
    return x * lax.rsqrt(jnp.mean(x * x, axis=-1, keepdims=True) + EPS)


def _modnorm(x, g, shift, scale):
    return _rms(x) * g * (1.0 + scale) + shift


def _cparams(n):
    return pltpu.CompilerParams(dimension_semantics=("arbitrary",) * n, vmem_limit_bytes=VMEM_LIMIT)


def _mod_spec(tm, tokens_per_mod):
    return pl.BlockSpec((1, 1, D_MODEL), lambda i, *_: ((i * tm) // tokens_per_mod, 0, 0))


def _row_spec(tm, width):
    return pl.BlockSpec((tm, width), lambda i: (i, 0))


def _full_spec(shape):
    return pl.BlockSpec(shape, lambda *_: (0,) * len(shape))


def _mod_kernel(c_ref, w_ref, b_ref, o_ref):
    c = c_ref[...]
    o_ref[...] = _bdot(c * _sigmoid(c), w_ref[...]) + b_ref[...]


def modulation_pallas(cvecs, w_mod, b_mod):
    R, D = cvecs.shape
    N = w_mod.shape[1]
    return pl.pallas_call(
        _mod_kernel, grid=(N // MOD_TN,),
        in_specs=[pl.BlockSpec((R, D), lambda j: (0, 0)), pl.BlockSpec((D, MOD_TN), lambda j: (0, j)),
                  pl.BlockSpec((1, MOD_TN), lambda j: (0, j))],
        out_specs=pl.BlockSpec((R, MOD_TN), lambda j: (0, j)),
        out_shape=jax.ShapeDtypeStruct((R, N), jnp.float32),
        compiler_params=_cparams(1), name="modulation",
    )(cvecs, w_mod, b_mod.reshape(1, N))


def _ab_in_kernel(x_ref, sh_ref, sc_ref, g_ref, w_ref, wl_ref, wlt_ref,
                  qkv_ref, gate_ref, qb_ref, kb_ref, vb_ref, ab_ref, abt_ref):
    hb = _modnorm(x_ref[...], g_ref[...], sh_ref[0], sc_ref[0]).astype(jnp.bfloat16)
    res = jnp.dot(hb, w_ref[...], preferred_element_type=jnp.float32)
    o = 0
    for ref, wd in ((qkv_ref, QKV_W), (gate_ref, A_W), (qb_ref, QB_W), (kb_ref, KV_W), (vb_ref, KV_W)):
        ref[...] = res[:, o:o + wd]
        o += wd
    ab_ref[...] = jnp.dot(hb, wl_ref[...], preferred_element_type=jnp.float32)
    abt_ref[...] = lax.dot_general(wlt_ref[...], hb, (((1,), (1,)), ((), ())), preferred_element_type=jnp.float32)


def ab_in_pallas(x, shift, scale, g, w_main, w_log_pad, w_log_t):
    N, D = x.shape
    tm = PROJ_TM
    ms = _mod_spec(tm, N // shift.shape[0])
    outs = [(N, QKV_W), (N, A_W), (N, QB_W), (N, KV_W), (N, KV_W), (N, LANE)]
    return pl.pallas_call(
        _ab_in_kernel, grid=(N // tm,),
        in_specs=[_row_spec(tm, D), ms, ms, _full_spec((1, D)), _full_spec((D, AB_MAIN)), _full_spec((D, LANE)),
                  _full_spec((AB_COLS, D))],
        out_specs=[_row_spec(tm, s[1]) for s in outs] + [pl.BlockSpec((AB_COLS, tm), lambda i: (0, i))],
        out_shape=[jax.ShapeDtypeStruct(s, jnp.float32) for s in outs] + [jax.ShapeDtypeStruct((AB_COLS, N), jnp.float32)],
        compiler_params=_cparams(1), name="ab_in",
    )(x, shift, scale, g.reshape(1, D), w_main, w_log_pad, w_log_t)


def _ab_out_kernel(oa_ref, ob_ref, w_ref, x_ref, gt_ref, sh_ref, sc_ref, g_ref, xn_ref, hn_ref):
    res = _bdot(oa_ref[...], w_ref[0:A_W, :]) + _bdot(ob_ref[...], w_ref[A_W:, :])
    xn = x_ref[...] + gt_ref[0] * res
    xn_ref[...] = xn
    hn_ref[...] = _modnorm(xn, g_ref[...], sh_ref[0], sc_ref[0])


def ab_out_pallas(oa, ob, w_out_b, x, gate, shift, scale, g):
    N, D = x.shape
    tm = PROJ_TM
    ms = _mod_spec(tm, N // gate.shape[0])
    return pl.pallas_call(
        _ab_out_kernel, grid=(N // tm,),
        in_specs=[_row_spec(tm, A_W), _row_spec(tm, QB_W), _full_spec((A_W + QB_W, D)), _row_spec(tm, D), ms, ms, ms,
                  _full_spec((1, D))],
        out_specs=[_row_spec(tm, D), _row_spec(tm, D)],
        out_shape=[jax.ShapeDtypeStruct((N, D), jnp.float32)] * 2,
        compiler_params=_cparams(1), name="ab_out",
    )(oa, ob, w_out_b, x, gate, shift, scale, g.reshape(1, D))


def _c_in_kernel(x_ref, sh_ref, sc_ref, g_ref, w_ref, vg_ref, u_ref, vn_ref):
    hb = _modnorm(x_ref[...], g_ref[...], sh_ref[0], sc_ref[0]).astype(jnp.bfloat16)
    res = _gelu_tanh(jnp.dot(hb, w_ref[...], preferred_element_type=jnp.float32))
    u_ref[...] = res[:, :C_WIDTH]
    vn_ref[...] = _rms(res[:, C_WIDTH:]) * vg_ref[...]


def c_in_pallas(x, shift, scale, g, w_in_b, vnorm_g):
    N, D = x.shape
    tm = PROJ_TM
    ms = _mod_spec(tm, N // shift.shape[0])
    return pl.pallas_call(
        _c_in_kernel, grid=(N // tm,),
        in_specs=[_row_spec(tm, D), ms, ms, _full_spec((1, D)), _full_spec((D, 2 * C_WIDTH)), _full_spec((1, C_WIDTH))],
        out_specs=[_row_spec(tm, C_WIDTH), _row_spec(tm, C_WIDTH)],
        out_shape=[jax.ShapeDtypeStruct((N, C_WIDTH), jnp.float32)] * 2,
        compiler_params=_cparams(1), name="c_in",
    )(x, shift, scale, g.reshape(1, D), w_in_b, vnorm_g.reshape(1, C_WIDTH))


def _c_mix_kernel(u_ref, vn_ref, ws_ref, bs_ref, w_ref, x_ref, gt_ref, sh_ref, sc_ref, g_ref, xn_ref, hn_ref, z_ref):
    tm = u_ref.shape[0]
    for cc in range(tm // C_CHUNK):
        rows = slice(cc * C_CHUNK, (cc + 1) * C_CHUNK)
        for grp in range(C_GROUPS):
            cols = slice(grp * C_GROUP_W, (grp + 1) * C_GROUP_W)
            mixed = _bdot(ws_ref[grp], vn_ref[rows, cols]) + bs_ref[:, grp:grp + 1]
            z_ref[rows, cols] = (u_ref[rows, cols] * mixed).astype(jnp.bfloat16)
    res = jnp.dot(z_ref[...], w_ref[...], preferred_element_type=jnp.float32)
    xn = x_ref[...] + gt_ref[0] * res
    xn_ref[...] = xn
    hn_ref[...] = _modnorm(xn, g_ref[...], sh_ref[0], sc_ref[0])


def c_mix_pallas(u, vn, w_s_b, bs, w_out_b, x, gate, shift, scale, g):
    N, D = x.shape
    tm = PROJ_TM
    ms = _mod_spec(tm, N // gate.shape[0])
    return pl.pallas_call(
        _c_mix_kernel, grid=(N // tm,),
        in_specs=[_row_spec(tm, C_WIDTH), _row_spec(tm, C_WIDTH), _full_spec((C_GROUPS, C_CHUNK, C_CHUNK)),
                  _full_spec((C_CHUNK, LANE)), _full_spec((C_WIDTH, D)), _row_spec(tm, D), ms, ms, ms, _full_spec((1, D))],
        out_specs=[_row_spec(tm, D), _row_spec(tm, D)],
        out_shape=[jax.ShapeDtypeStruct((N, D), jnp.float32)] * 2,
        scratch_shapes=[pltpu.VMEM((tm, C_WIDTH), jnp.bfloat16)],
        compiler_params=_cparams(1), name="c_mix",
    )(u, vn, w_s_b, bs, w_out_b, x, gate, shift, scale, g.reshape(1, D))


def _unit_tri_inverse_offdiag(a, ii, jj):
    def same_block(b):
        return jnp.right_shift(ii, b) == jnp.right_shift(jj, b)
    n = -jnp.where(same_block(1), a, 0.0)
    for b in range(1, 7):
        o = jnp.where(same_block(b + 1) & jnp.logical_not(same_block(b)), a, 0.0)
        p = o + _bdot(n, o)
        q = p + _bdot(p, n)
        n = n - q
    return n


DELTA_GROUP = 2


def _delta_kernel(q_ref, k_ref, v_ref, gate_ref, ab_ref, abt_ref, cw_ref, prow_ref, pcol_ref, og_ref, s0_ref,
                  o_ref, sfin_ref, u_s, w_s, qk_s, qd_s, ktt_s, egl_s, of_s, ob_s, *, use_s0):
    T = q_ref.shape[1]
    nck = T // DC
    h = pl.program_id(1)
    ii = lax.broadcasted_iota(jnp.int32, (DC, DC), 0)
    jj = lax.broadcasted_iota(jnp.int32, (DC, DC), 1)
    low = (jj <= ii).astype(jnp.float32)
    upp = (jj >= ii).astype(jnp.float32)
    r16 = lax.broadcasted_iota(jnp.int32, (AB_COLS, DC), 0)
    i2 = lax.broadcasted_iota(jnp.int32, (2 * DC, 2 * DC), 0)
    j2 = lax.broadcasted_iota(jnp.int32, (2 * DC, 2 * DC), 1)

    def shifted(ref, c, d):
        cur = ref[0, pl.ds(pl.multiple_of(c * DC, DC), DC), :]
        if d == 0:
            return cur
        step = 1 if d > 0 else -1
        cn = jnp.clip(c + step, 0, nck - 1)
        nb = ref[0, pl.ds(pl.multiple_of(cn * DC, DC), DC), :]
        ok = jnp.where((c + step >= 0) & (c + step < nck), 1.0, 0.0)
        rc = pltpu.roll(cur, (-d) % DC, axis=0)
        rn = pltpu.roll(nb, (-d) % DC, axis=0) * ok
        return jnp.where((ii + d >= 0) & (ii + d < DC), rc, rn)

    def prep_chunk(c):
        rows = pl.ds(pl.multiple_of(c * DC, DC), DC)

        def conv_silu(ref, idx):
            acc = jnp.zeros((DC, HEAD_DIM), jnp.float32)
            for kk in range(A_CONV):
                acc = acc + shifted(ref, c, kk - A_CONV // 2) * cw_ref[0, idx, kk:kk + 1, :]
            return acc * _sigmoid(acc)

        q = conv_silu(q_ref, 0)
        k = conv_silu(k_ref, 1)
        v = conv_silu(v_ref, 2)
        q = q * lax.rsqrt(jnp.sum(q * q, axis=-1, keepdims=True) + EPS) * (HEAD_DIM ** -0.5)
        k = k * lax.rsqrt(jnp.sum(k * k, axis=-1, keepdims=True) + EPS)
        kt = k.T
        ab = ab_ref[0, rows, :]
        g_all = -prow_ref[0:1, :] * _softplus(ab + prow_ref[1:2, :])
        beta_all = _sigmoid(ab)
        abt = abt_ref[:, rows]
        gr_all = -pcol_ref[0] * _softplus(abt + pcol_ref[1])

        def pick_col(x, col):
            return jnp.sum(jnp.where(jj == col, x, 0.0), axis=1, keepdims=True)

        def pick_row(x, row):
            return jnp.sum(jnp.where(r16 == row, x, 0.0), axis=0, keepdims=True)

        qkt = _bdot(q, kt)
        per_dir = []
        for dirn in range(2):
            col = dirn * A_HEADS + h
            gc = pick_col(_hdot(low if dirn == 0 else upp, g_all), col)
            gr = pick_row(_hdot(gr_all, upp if dirn == 0 else low), col)
            beta = pick_col(beta_all, 2 * A_HEADS + col)
            tri = (jj <= ii) if dirn == 0 else (jj >= ii)
            gamma = jnp.where(tri, jnp.exp(jnp.where(tri, gc - gr, 0.0)), 0.0)
            kb = k * beta
            egc = jnp.exp(gc)
            gl = gc[DC - 1:DC, :] if dirn == 0 else gc[0:1, :]
            qk_s[dirn, c] = qkt * gamma
            qd_s[dirn, c] = q * egc
            ktt_s[dirn, c] = kt * jnp.exp(gl - gr)
            egl_s[dirn, c] = jnp.broadcast_to(jnp.exp(gl), (8, DC))
            per_dir.append((kb, gamma, jnp.concatenate([v * beta, kb * egc], axis=1)))
        kk = _bdot(jnp.concatenate([per_dir[0][0], per_dir[1][0]], axis=0), kt)
        zero = jnp.zeros((DC, DC), jnp.float32)
        a_f = jnp.where(jj < ii, kk[:DC] * per_dir[0][1], 0.0)
        a_b = jnp.where(jj > ii, kk[DC:] * per_dir[1][1], 0.0)
        a = jnp.concatenate([jnp.concatenate([a_f, zero], axis=1), jnp.concatenate([zero, a_b], axis=1)], axis=0)
        n = _unit_tri_inverse_offdiag(a, i2, j2)
        rhs = jnp.concatenate([per_dir[0][2], per_dir[1][2]], axis=0)
        sol = rhs + _bdot(n, rhs)
        for dirn in range(2):
            u_s[dirn, c] = sol[dirn * DC:(dirn + 1) * DC, :HEAD_DIM]
            w_s[dirn, c] = sol[dirn * DC:(dirn + 1) * DC, HEAD_DIM:]

    def prep(cg, carry):
        for g in range(DELTA_GROUP):
            prep_chunk(cg * DELTA_GROUP + g)
        return carry

    lax.fori_loop(0, nck // DELTA_GROUP, prep, 0)

    def chunk_update(S, c, dirn):
        v_new = u_s[dirn, c] - _bdot(w_s[dirn, c], S)
        o = _bdot(qd_s[dirn, c], S) + _bdot(qk_s[dirn, c], v_new)
        S = S * egl_s[dirn, c, 0:1, :] + _bdot(ktt_s[dirn, c], v_new)
        return S, o

    def step(i, carry):
        sf, sb = carry
        sf, o_f = chunk_update(sf, i, 0)
        of_s[pl.ds(pl.multiple_of(i * DC, DC), DC), :] = o_f
        cb = nck - 1 - i
        sb, o_b = chunk_update(sb, cb, 1)
        ob_s[pl.ds(pl.multiple_of(cb * DC, DC), DC), :] = o_b
        return sf, sb

    if use_s0:
        init = (s0_ref[0, 0, 0], s0_ref[0, 1, 0])
    else:
        init = (jnp.zeros((HEAD_DIM, HEAD_DIM), jnp.float32), jnp.zeros((HEAD_DIM, HEAD_DIM), jnp.float32))
    sf, sb = lax.fori_loop(0, nck, step, init)
    sfin_ref[0, 0, 0] = sf
    sfin_ref[0, 1, 0] = sb

    def fin(c, carry):
        rows = pl.ds(pl.multiple_of(c * DC, DC), DC)
        o = of_s[rows, :] + ob_s[rows, :]
        o = _rms(o) * og_ref[...]
        g = gate_ref[0, rows, :]
        o_ref[0, rows, :] = o * (g * _sigmoid(g))
        return carry

    lax.fori_loop(0, nck, fin, 0)


def delta_mixer_pallas(qkv, gate, ab, abt, cw, prow, pcol, onorm_g, s0):
    B, T, _ = qkv.shape
    H = A_HEADS
    state_block = (1, 2, 1, HEAD_DIM, HEAD_DIM)
    use_s0 = s0 is not None
    if use_s0:
        s0_spec = pl.BlockSpec(state_block, lambda b, h: (b, 0, h, 0, 0))
    else:
        s0 = jnp.zeros(state_block, jnp.float32)
        s0_spec = _full_spec(state_block)
    nck = T // DC
    seq = lambda off: pl.BlockSpec((1, T, HEAD_DIM), lambda b, h: (b, 0, off + h))
    assert nck % DELTA_GROUP == 0
    tok = pltpu.VMEM((T, HEAD_DIM), jnp.float32)
    blk = pltpu.VMEM((2, nck, DC, HEAD_DIM), jnp.float32)
    return pl.pallas_call(
        functools.partial(_delta_kernel, use_s0=use_s0),
        grid=(B, H),
        in_specs=[seq(0), seq(H), seq(2 * H), seq(0),
                  pl.BlockSpec((1, T, LANE), lambda b, h: (b, 0, 0)), pl.BlockSpec((AB_COLS, T), lambda b, h: (0, b)),
                  pl.BlockSpec((1, 3, A_CONV, HEAD_DIM), lambda b, h: (h, 0, 0, 0)),
                  _full_spec((2, LANE)), _full_spec((2, AB_COLS, LANE)), _full_spec((1, HEAD_DIM)), s0_spec],
        out_specs=[seq(0), pl.BlockSpec(state_block, lambda b, h: (b, 0, h, 0, 0))],
        out_shape=[jax.ShapeDtypeStruct((B, T, A_W), jnp.float32),
                   jax.ShapeDtypeStruct((B, 2, H, HEAD_DIM, HEAD_DIM), jnp.float32)],
        scratch_shapes=[blk, blk, blk, blk, blk, pltpu.VMEM((2, nck, 8, DC), jnp.float32), tok, tok],
        compiler_params=_cparams(2), name="delta",
    )(qkv, qkv, qkv, gate, ab, abt, cw, prow, pcol, onorm_g, s0)


def _softmax_segments(scores, values, sink):
    m = jnp.full((scores[0].shape[0], 1), sink, jnp.float32)
    for s in scores:
        m = jnp.maximum(m, jnp.max(s, axis=-1, keepdims=True))
    den = jnp.exp(sink - m)
    o = None
    for s, v in zip(scores, values):
        p = jnp.exp(s - m)
        den = den + jnp.sum(p, axis=-1, keepdims=True)
        pv = _bdot(p, v)
        o = pv if o is None else o + pv
    return o / den


def _ctx_attn_kernel(sink_ref, q_ref, k_ref, v_ref, o_ref):
    hkv = pl.program_id(1)
    k = k_ref[0]
    v = v_ref[0]
    for g in range(B_GROUP):
        q = q_ref[0, :, g * HEAD_DIM:(g + 1) * HEAD_DIM]
        s = _bdot_nt(q, k) * (HEAD_DIM ** -0.5)
        o_ref[0, :, g * HEAD_DIM:(g + 1) * HEAD_DIM] = _softmax_segments([s], [v], sink_ref[hkv * B_GROUP + g])


def context_attention_pallas(qb, kb, vb, sink):
    B, S, _ = qb.shape
    gw = B_GROUP * HEAD_DIM
    return pl.pallas_call(
        _ctx_attn_kernel,
        grid=(B, B_KV_HEADS),
        in_specs=[pl.BlockSpec(memory_space=pltpu.SMEM),
                  pl.BlockSpec((1, S, gw), lambda b, h: (b, 0, h)),
                  pl.BlockSpec((1, S, HEAD_DIM), lambda b, h: (b, 0, h)),
                  pl.BlockSpec((1, S, HEAD_DIM), lambda b, h: (b, 0, h))],
        out_specs=pl.BlockSpec((1, S, gw), lambda b, h: (b, 0, h)),
        out_shape=jax.ShapeDtypeStruct(qb.shape, jnp.float32),
        compiler_params=_cparams(2), name="ctx_attn",
    )(sink, qb, kb, vb)


def rope_tables(n_tokens):
    t = np.arange((n_tokens // GRID_W) * GRID_W)
    row = jnp.asarray((t // GRID_W).astype(np.float32))
    col = jnp.asarray((t % GRID_W).astype(np.float32))
    nf = HEAD_DIM // 4
    inv = ROPE_BASE ** (-jnp.arange(nf, dtype=jnp.float32) / nf)
    ar = row[:, None] * inv
    ac = col[:, None] * inv
    rc = jnp.concatenate([jnp.cos(ar), jnp.cos(ar), jnp.cos(ac), jnp.cos(ac)], axis=-1)
    rs = jnp.concatenate([-jnp.sin(ar), jnp.sin(ar), -jnp.sin(ac), jnp.sin(ac)], axis=-1)
    return rc, rs


def _rope(x, rc, rs):
    lane = lax.broadcasted_iota(jnp.int32, x.shape, 1)
    nf = HEAD_DIM // 4
    swapped = jnp.where((lane & (2 * nf - 1)) < nf, pltpu.roll(x, HEAD_DIM - nf, axis=1), pltpu.roll(x, nf, axis=1))
    return x * rc + swapped * rs


def _lat_attn_kernel(sink_ref, q_ref, kp_ref, kc_ref, kn_ref, vp_ref, vc_ref, vn_ref, kx_ref, vx_ref,
                     rcp_ref, rcc_ref, rcn_ref, rsp_ref, rsc_ref, rsn_ref, o_ref):
    hkv = pl.program_id(1)
    i = pl.program_id(2)
    nb = pl.num_programs(2)
    r = lax.broadcasted_iota(jnp.int32, (ATTN_BLOCK, ATTN_BLOCK), 0)
    c = lax.broadcasted_iota(jnp.int32, (ATTN_BLOCK, ATTN_BLOCK), 1)
    kp = _rope(kp_ref[0], rcp_ref[...], rsp_ref[...])
    kc = _rope(kc_ref[0], rcc_ref[...], rsc_ref[...])
    kn = _rope(kn_ref[0], rcn_ref[...], rsn_ref[...])
    ok_p = (c >= r) & (i > 0)
    ok_n = (c <= r) & (i < nb - 1)
    scale = HEAD_DIM ** -0.5
    for g in range(B_GROUP):
        q = _rope(q_ref[0, :, g * HEAD_DIM:(g + 1) * HEAD_DIM], rcc_ref[...], rsc_ref[...])
        sx = _bdot_nt(q, kx_ref[0]) * scale
        sp = jnp.where(ok_p, _bdot_nt(q, kp) * scale, NEG_INF)
        sc = _bdot_nt(q, kc) * scale
        sn = jnp.where(ok_n, _bdot_nt(q, kn) * scale, NEG_INF)
        o_ref[0, :, g * HEAD_DIM:(g + 1) * HEAD_DIM] = _softmax_segments(
            [sx, sp, sc, sn], [vx_ref[0], vp_ref[0], vc_ref[0], vn_ref[0]], sink_ref[hkv * B_GROUP + g])


def latent_attention_pallas(qb, kb, vb, k_ctx, v_ctx, sink):
    assert WINDOW == ATTN_BLOCK
    B, T, _ = qb.shape
    P = k_ctx.shape[1]
    nb = T // ATTN_BLOCK
    gw = B_GROUP * HEAD_DIM
    rc, rs = rope_tables(T)
    prev = lambda b, h, i: (b, jnp.maximum(i - 1, 0), h)
    cur = lambda b, h, i: (b, i, h)
    nxt = lambda b, h, i: (b, jnp.minimum(i + 1, nb - 1), h)
    blk = lambda f: pl.BlockSpec((1, ATTN_BLOCK, HEAD_DIM), f)
    tab = lambda f: pl.BlockSpec((ATTN_BLOCK, HEAD_DIM), lambda b, h, i: (f(b, h, i)[1], 0))
    ctx = pl.BlockSpec((1, P, HEAD_DIM), lambda b, h, i: (b, 0, h))
    return pl.pallas_call(
        _lat_attn_kernel,
        grid=(B, B_KV_HEADS, nb),
        in_specs=[pl.BlockSpec(memory_space=pltpu.SMEM), pl.BlockSpec((1, ATTN_BLOCK, gw), cur),
                  blk(prev), blk(cur), blk(nxt), blk(prev), blk(cur), blk(nxt), ctx, ctx,
                  tab(prev), tab(cur), tab(nxt), tab(prev), tab(cur), tab(nxt)],
        out_specs=pl.BlockSpec((1, ATTN_BLOCK, gw), cur),
        out_shape=jax.ShapeDtypeStruct(qb.shape, jnp.float32),
        compiler_params=_cparams(3), name="lat_attn",
    )(sink, qb, kb, kb, kb, vb, vb, vb, k_ctx, v_ctx, rc, rc, rc, rs, rs, rs)


def _top16(s, iota, iota16, want_rank):
    cur = s
    rank = jnp.full(s.shape, float(PEER_TOPK), jnp.float32) if want_rank else None
    sv = jnp.zeros((PEER_TOPK, s.shape[1]), jnp.float32)
    si = jnp.zeros((PEER_TOPK, s.shape[1]), jnp.float32)
    for k in range(PEER_TOPK):
        m = jnp.max(cur, axis=0, keepdims=True)
        idx = jnp.min(jnp.where(cur == m, iota, float(N_KEYS)), axis=0, keepdims=True)
        sel = iota == idx
        if want_rank:
            rank = jnp.where(sel, float(k), rank)
        else:
            si = jnp.where(iota16 == float(k), idx, si)
        cur = jnp.where(sel, -jnp.inf, cur)
        sv = jnp.where(iota16 == float(k), m, sv)
    return sv, (rank if want_rank else si)


def _staircase(sv0, sv1, iota16):
    n = jnp.zeros_like(sv0)
    f = sv0 + sv1[0:1, :]
    top = f[0:1, :]
    z = jnp.zeros_like(top)
    for _ in range(PEER_TOPK):
        fm = jnp.max(f, axis=0, keepdims=True)
        cand = iota16 * float(PEER_TOPK) + n
        ci = jnp.min(jnp.where(f == fm, cand, 1e9), axis=0, keepdims=True)
        win = cand == ci
        z = z + jnp.exp(fm - top)
        n = jnp.where(win, n + 1.0, n)
        nw = jnp.max(jnp.where(win, n, 0.0), axis=0, keepdims=True)
        nxt = jnp.max(jnp.where(iota16 == nw, sv1, -jnp.inf), axis=0, keepdims=True)
        f = jnp.where(win, sv0 + nxt, f)
    return n, z


def _peer_route_kernel(h_ref, wq_ref, keys_ref, ht_ref, r1_ref, p1_ref, c0_ref, q0_ref, q_scr, s_scr):
    tm = h_ref.shape[0]
    hf = h_ref[...]
    ht_ref[...] = hf.T.astype(jnp.bfloat16)
    q_scr[...] = jnp.dot(hf.astype(jnp.bfloat16), wq_ref[...], preferred_element_type=jnp.float32)

    def head_body(h, carry):
        for p in range(2):
            off = pl.multiple_of(h * D_KEY + p * (D_KEY // 2), LANE)
            qhp = q_scr[:, pl.ds(off, D_KEY // 2)]
            s_scr[p] = lax.dot_general(keys_ref[p], qhp, (((1,), (1,)), ((), ())), precision=lax.Precision.HIGHEST,
                                       preferred_element_type=jnp.float32)

        def chunk_body(c, carry2):
            lanes = pl.ds(pl.multiple_of(c * LANE, LANE), LANE)
            iota = lax.broadcasted_iota(jnp.int32, (N_KEYS, LANE), 0).astype(jnp.float32)
            iota16 = lax.broadcasted_iota(jnp.int32, (PEER_TOPK, LANE), 0).astype(jnp.float32)
            s0 = s_scr[0, :, lanes]
            s1 = s_scr[1, :, lanes]
            sv0, si0 = _top16(s0, iota, iota16, False)
            sv1, rank1 = _top16(s1, iota, iota16, True)
            n, z = _staircase(sv0, sv1, iota16)
            c0 = jnp.zeros_like(s0)
            for k in range(PEER_TOPK):
                c0 = jnp.where(iota == si0[k:k + 1, :], n[k:k + 1, :], c0)
            p1 = jnp.exp(s1 - sv1[0:1, :])
            r1_ref[h, :, lanes] = pltpu.bitcast(rank1.astype(jnp.bfloat16), jnp.uint32)
            p1_ref[h, :, lanes] = pltpu.bitcast(p1.astype(jnp.bfloat16), jnp.uint32)
            c0_ref[h, :, lanes] = c0
            q0_ref[h, :, lanes] = jnp.exp(s0 - sv0[0:1, :]) / z
            return carry2

        lax.fori_loop(0, tm // LANE, chunk_body, 0, unroll=4)
        return carry

    lax.fori_loop(0, PEER_HEADS, head_body, 0)


def _peer_expert_kernel(ht_ref, r1_ref, p1_ref, c0_ref, q0_ref, ed_ref, eut_ref, x_ref, gt_ref, fg_ref, out_ref,
                        acc_ref, act_ref, wg_ref, *, final_norm):
    e = pl.program_id(1)
    tm = ht_ref.shape[1]
    te = ed_ref.shape[0]
    bf16 = jnp.bfloat16

    @pl.when(e == 0)
    def _():
        acc_ref[...] = jnp.zeros_like(acc_ref)

    for sb in range(te // EXPERT_SUB):
        rs = slice(sb * EXPERT_SUB, (sb + 1) * EXPERT_SUB)
        act_ref[rs, :] = jnp.dot(ed_ref[rs, :], ht_ref[...], preferred_element_type=jnp.float32)
    half = N_KEYS // 2
    for ii in range(te // N_KEYS):
        for c in range(tm // LANE):
            lanes = slice(c * LANE, (c + 1) * LANE)
            g = jnp.zeros((N_KEYS, LANE), bf16)
            for h in range(PEER_HEADS):
                c0b = jnp.broadcast_to(c0_ref[h, ii:ii + 1, lanes], (N_KEYS, LANE)).astype(bf16)
                q0b = jnp.broadcast_to(q0_ref[h, ii:ii + 1, lanes], (N_KEYS, LANE)).astype(bf16)
                r1b = pltpu.bitcast(r1_ref[h, :, lanes], bf16)
                p1b = pltpu.bitcast(p1_ref[h, :, lanes], bf16)
                g = g + jnp.where(r1b < c0b, p1b, jnp.zeros((), bf16)) * q0b
            ab = _gelu_tanh(act_ref[ii * N_KEYS:(ii + 1) * N_KEYS, lanes]).astype(bf16)
            wg_ref[ii * half:(ii + 1) * half, lanes] = pltpu.bitcast(g * ab, jnp.uint32)
    acc_ref[...] += jnp.dot(eut_ref[...], pltpu.bitcast(wg_ref[...], bf16), preferred_element_type=jnp.float32)

    @pl.when(e == pl.num_programs(1) - 1)
    def _():
        y = x_ref[...] + gt_ref[0] * acc_ref[...].T
        if final_norm:
            y = _rms(y) * fg_ref[...]
        out_ref[...] = y


def peer_pallas(h, x, gate, wq_b, keys, ed_b, eut_b, final_g, final_norm):
    T, D = h.shape
    nq = PEER_HEADS * D_KEY
    route_shape = jax.ShapeDtypeStruct((PEER_HEADS, N_KEYS, T), jnp.float32)
    route_shape_b = jax.ShapeDtypeStruct((PEER_HEADS, N_KEYS // 2, T), jnp.uint32)
    tmr = ROUTE_TM
    route_spec = pl.BlockSpec((PEER_HEADS, N_KEYS, tmr), lambda i: (0, 0, i))
    route_spec_b = pl.BlockSpec((PEER_HEADS, N_KEYS // 2, tmr), lambda i: (0, 0, i))
    ht, r1, p1, c0, q0 = pl.pallas_call(
        _peer_route_kernel,
        grid=(T // tmr,),
        in_specs=[_row_spec(tmr, D), _full_spec((D, nq)), _full_spec((2, N_KEYS, D_KEY // 2))],
        out_specs=[pl.BlockSpec((D, tmr), lambda i: (0, i)), route_spec_b, route_spec_b, route_spec, route_spec],
        out_shape=[jax.ShapeDtypeStruct((D, T), jnp.bfloat16), route_shape_b, route_shape_b, route_shape, route_shape],
        scratch_shapes=[pltpu.VMEM((tmr, nq), jnp.float32), pltpu.VMEM((2, N_KEYS, tmr), jnp.float32)],
        compiler_params=_cparams(1), name="peer_route",
    )(h, wq_b, keys)

    tm, te = EXPERT_TM, EXPERT_TE
    packed_spec = pl.BlockSpec((PEER_HEADS, N_KEYS // 2, tm), lambda i, e: (0, 0, i))
    row_spec = pl.BlockSpec((PEER_HEADS, te // N_KEYS, tm), lambda i, e: (0, e, i))
    return pl.pallas_call(
        functools.partial(_peer_expert_kernel, final_norm=final_norm),
        grid=(T // tm, N_EXPERTS // te),
        in_specs=[pl.BlockSpec((D, tm), lambda i, e: (0, i)), packed_spec, packed_spec, row_spec, row_spec,
                  pl.BlockSpec((te, D), lambda i, e: (e, 0)), pl.BlockSpec((D, te), lambda i, e: (0, e)),
                  pl.BlockSpec((tm, D), lambda i, e: (i, 0)), _mod_spec(tm, T // gate.shape[0]), _full_spec((1, D))],
        out_specs=pl.BlockSpec((tm, D), lambda i, e: (i, 0)),
        out_shape=jax.ShapeDtypeStruct((T, D), jnp.float32),
        scratch_shapes=[pltpu.VMEM((D, tm), jnp.float32), pltpu.VMEM((te, tm), jnp.float32), pltpu.VMEM((te // 2, tm), jnp.uint32)],
        compiler_params=_cparams(2), name="peer_expert",
    )(ht, r1, p1, c0, q0, ed_b, eut_b, x, gate, final_g.reshape(1, D))


def kernel(x_prompt, x_sample, cache_attn_k, cache_attn_v, state_delta, c, c_ctx, w_mod, b_mod, norm_g, w_in_ab, conv_ab, a_log_ab, dt_bias_ab, onorm_ab, sink_ab, w_out_ab, w_in_c, vnorm_c, w_s_c, b_s_c, w_out_c, peer_wq, peer_keys, peer_down, peer_up, final_norm_g):
    f32, bf16 = jnp.float32, jnp.bfloat16
    BP, SP, D = x_prompt.shape
    BS, TS, _ = x_sample.shape
    past = cache_attn_k.shape[2]
    H = A_HEADS
    cvecs = jnp.zeros((8, D), f32).at[0].set(c_ctx).at[1:1 + BS].set(c)
    streams = [dict(x=x_prompt.reshape(BP * SP, D), B=BP, T=SP, rows=slice(0, 1)),
               dict(x=x_sample.reshape(BS * TS, D), B=BS, T=TS, rows=slice(1, 1 + BS))]
    new_k = new_v = new_s = None
    for layer in range(DEPTH):
        m = modulation_pallas(cvecs, w_mod[layer], b_mod[layer])
        li = layer // 2
        last = layer == DEPTH - 1
        wq_b = peer_wq[layer].astype(bf16)
        ed_b = peer_down[layer].astype(bf16)
        eut_b = peer_up[layer].T.astype(bf16)
        if layer % 2 == 0:
            w_in = w_in_ab[li]
            c0 = QKV_W + A_W
            w_main = jnp.concatenate([w_in[:, :c0], w_in[:, c0 + AB_COLS:]], axis=1).astype(bf16)
            w_log = w_in[:, c0:c0 + AB_COLS].astype(bf16)
            w_log_pad = jnp.pad(w_log, ((0, 0), (0, LANE - AB_COLS)))
            w_out_b = w_out_ab[li].astype(bf16)
            cw = conv_ab[li].reshape(A_CONV, 3, H, HEAD_DIM).transpose(2, 1, 0, 3)
            a_dec = jnp.exp(a_log_ab[li].astype(f32)).reshape(2 * H)
            dt = dt_bias_ab[li].astype(f32).reshape(2 * H)
            prow = jnp.zeros((2, LANE), f32).at[0, :2 * H].set(a_dec).at[1, :2 * H].set(dt)
            pcol = jnp.zeros((2, AB_COLS, LANE), f32).at[0, :2 * H].set(a_dec[:, None]).at[1, :2 * H].set(dt[:, None])
            og = onorm_ab[li].reshape(1, HEAD_DIM).astype(f32)
            sink = sink_ab[li].astype(f32)
        else:
            w_in_b = w_in_c[li].astype(bf16)
            w_s_b = w_s_c[li].astype(bf16)
            bs = jnp.pad(b_s_c[li].T.astype(f32), ((0, 0), (0, LANE - C_GROUPS)))
            w_out_b = w_out_c[li].astype(bf16)
        for si, st in enumerate(streams):
            B, T = st["B"], st["T"]
            mk = [m[st["rows"], k * D:(k + 1) * D].reshape(-1, 1, D) for k in range(6)]
            x = st["x"]
            if layer % 2 == 0:
                qkv, gate, qb, kb, vb, ab, abt = ab_in_pallas(x, mk[0], mk[1], norm_g[layer, 0], w_main, w_log_pad, w_log.T)
                s0 = None if si == 0 else state_delta[:, li]
                o_a, s_fin = delta_mixer_pallas(qkv.reshape(B, T, QKV_W), gate.reshape(B, T, A_W), ab.reshape(B, T, LANE),
                                                abt, cw, prow, pcol, og, s0)
                if si == 0:
                    o_b = context_attention_pallas(qb.reshape(B, T, QB_W), kb.reshape(B, T, KV_W), vb.reshape(B, T, KV_W), sink)
                    new_k = kb.reshape(B, 1, T, B_KV_HEADS, HEAD_DIM)
                    new_v = vb.reshape(B, 1, T, B_KV_HEADS, HEAD_DIM)
                    new_s = s_fin[:, None]
                else:
                    o_b = latent_attention_pallas(qb.reshape(B, T, QB_W), kb.reshape(B, T, KV_W), vb.reshape(B, T, KV_W),
                                                  cache_attn_k[:, li].reshape(B, past, KV_W),
                                                  cache_attn_v[:, li].reshape(B, past, KV_W), sink)
                x, hn = ab_out_pallas(o_a.reshape(B * T, A_W), o_b.reshape(B * T, QB_W), w_out_b, x, mk[2], mk[3], mk[4],
                                      norm_g[layer, 1])
            else:
                u, vn = c_in_pallas(x, mk[0], mk[1], norm_g[layer, 0], w_in_b, vnorm_c[li])
                x, hn = c_mix_pallas(u, vn, w_s_b, bs, w_out_b, x, mk[2], mk[3], mk[4], norm_g[layer, 1])
            st["x"] = peer_pallas(hn, x, mk[5], wq_b, peer_keys[layer], ed_b, eut_b, final_norm_g, last)
    y_prompt = streams[0]["x"].reshape(BP, SP, D)
    y_sample = streams[1]["x"].reshape(BS, TS, D)
    return (y_prompt, y_sample, new_k, new_v, new_s)
```

```python
import functools
import math

import numpy as np
import jax
import jax.numpy as jnp
from jax import lax
from jax.experimental import pallas as pl
from jax.experimental.pallas import tpu as pltpu

D_MODEL = 1024
DEPTH = 2
GRID_W = 64
HEAD_DIM = 128
EPS = 1e-6
NEG_INF = -1e30
A_HEADS = D_MODEL // (2 * HEAD_DIM)
A_W = A_HEADS * HEAD_DIM
A_CONV = 5
B_HEADS = D_MODEL // (2 * HEAD_DIM)
B_KV_HEADS = B_HEADS // 2
B_GROUP = B_HEADS // B_KV_HEADS
WINDOW = 128
ATTN_BLOCK = 128
ROPE_BASE = 10000.0
C_WIDTH = D_MODEL
C_GROUPS = 8
C_GROUP_W = C_WIDTH // C_GROUPS
C_CHUNK = 128
PEER_HEADS = 8
N_KEYS = 128
N_EXPERTS = N_KEYS * N_KEYS
PEER_TOPK = 16
D_KEY = 256

LANE = 128
VMEM_LIMIT = 48 * 1024 * 1024
PROJ_TM = 512
ROUTE_TM = 512
EXPERT_TM = 512
EXPERT_TE = 2048
EXPERT_SUB = 256
EXPERT_UP = 1024
MOD_TN = 1536
DC = 128

QKV_W = 3 * A_W
QB_W = B_HEADS * HEAD_DIM
KV_W = B_KV_HEADS * HEAD_DIM
AB_COLS = 4 * A_HEADS
AB_MAIN = QKV_W + A_W + QB_W + 2 * KV_W


def _bdot(a, b):
    return jnp.dot(a.astype(jnp.bfloat16), b.astype(jnp.bfloat16), preferred_element_type=jnp.float32)


def _bdot_nt(a, b):
    return lax.dot_general(a.astype(jnp.bfloat16), b.astype(jnp.bfloat16), (((1,), (1,)), ((), ())),
                           preferred_element_type=jnp.float32)


def _hdot(a, b):
    return jnp.dot(a, b, precision=lax.Precision.HIGHEST, preferred_element_type=jnp.float32)


def _sigmoid(x):
    return 1.0 / (1.0 + jnp.exp(-x))


def _softplus(x):
    return jnp.maximum(x, 0.0) + jnp.log(1.0 + jnp.exp(-jnp.abs(x)))


def _gelu_tanh(x):
    k0 = -2.0 * math.sqrt(2.0 / math.pi) * math.log2(math.e)
    return x / (1.0 + jnp.exp2(x * (k0 + (k0 * 0.044715) * (x * x))))


def _rms(x):
    return x * lax.rsqrt(jnp.mean(x * x, axis=-1, keepdims=True) + EPS)


def _modnorm(x, g, shift, scale):
    return _rms(x) * g * (1.0 + scale) + shift


def _cparams(n):
    return pltpu.CompilerParams(dimension_semantics=("arbitrary",) * n, vmem_limit_bytes=VMEM_LIMIT)


def _mod_spec(tm, tokens_per_mod):
    return pl.BlockSpec((1, 1, D_MODEL), lambda i, *_: ((i * tm) // tokens_per_mod, 0, 0))


def _row_spec(tm, width):
    return pl.BlockSpec((tm, width), lambda i: (i, 0))


def _full_spec(shape):
    return pl.BlockSpec(shape, lambda *_: (0,) * len(shape))


def _mod_kernel(c_ref, w_ref, b_ref, o_ref):
    c = c_ref[...]
    o_ref[...] = _bdot(c * _sigmoid(c), w_ref[...]) + b_ref[...]


def modulation_pallas(cvecs, w_mod, b_mod):
    R, D = cvecs.shape
    N = w_mod.shape[1]
    return pl.pallas_call(
        _mod_kernel, grid=(N // MOD_TN,),
        in_specs=[pl.BlockSpec((R, D), lambda j: (0, 0)), pl.BlockSpec((D, MOD_TN), lambda j: (0, j)),
                  pl.BlockSpec((1, MOD_TN), lambda j: (0, j))],
        out_specs=pl.BlockSpec((R, MOD_TN), lambda j: (0, j)),
        out_shape=jax.ShapeDtypeStruct((R, N), jnp.float32),
        compiler_params=_cparams(1), name="modulation",
    )(cvecs, w_mod, b_mod.reshape(1, N))


def _ab_in_kernel(x_ref, sh_ref, sc_ref, g_ref, w_ref, wl_ref, wlt_ref,
                  qkv_ref, gate_ref, qb_ref, kb_ref, vb_ref, ab_ref, abt_ref):
    hb = _modnorm(x_ref[...], g_ref[...], sh_ref[0], sc_ref[0]).astype(jnp.bfloat16)
    res = jnp.dot(hb, w_ref[...], preferred_element_type=jnp.float32)
    o = 0
    for ref, wd in ((qkv_ref, QKV_W), (gate_ref, A_W), (qb_ref, QB_W), (kb_ref, KV_W), (vb_ref, KV_W)):
        ref[...] = res[:, o:o + wd]
        o += wd
    ab_ref[...] = jnp.dot(hb, wl_ref[...], preferred_element_type=jnp.float32)
    abt_ref[...] = lax.dot_general(wlt_ref[...], hb, (((1,), (1,)), ((), ())), preferred_element_type=jnp.float32)


def ab_in_pallas(x, shift, scale, g, w_main, w_log_pad, w_log_t):
    N, D = x.shape
    tm = PROJ_TM
    ms = _mod_spec(tm, N // shift.shape[0])
    outs = [(N, QKV_W), (N, A_W), (N, QB_W), (N, KV_W), (N, KV_W), (N, LANE)]
    return pl.pallas_call(
        _ab_in_kernel, grid=(N // tm,),
        in_specs=[_row_spec(tm, D), ms, ms, _full_spec((1, D)), _full_spec((D, AB_MAIN)), _full_spec((D, LANE)),
                  _full_spec((AB_COLS, D))],
        out_specs=[_row_spec(tm, s[1]) for s in outs] + [pl.BlockSpec((AB_COLS, tm), lambda i: (0, i))],
        out_shape=[jax.ShapeDtypeStruct(s, jnp.float32) for s in outs] + [jax.ShapeDtypeStruct((AB_COLS, N), jnp.float32)],
        compiler_params=_cparams(1), name="ab_in",
    )(x, shift, scale, g.reshape(1, D), w_main, w_log_pad, w_log_t)


def _ab_out_kernel(oa_ref, ob_ref, w_ref, x_ref, gt_ref, sh_ref, sc_ref, g_ref, xn_ref, hn_ref):
    res = _bdot(oa_ref[...], w_ref[0:A_W, :]) + _bdot(ob_ref[...], w_ref[A_W:, :])
    xn = x_ref[...] + gt_ref[0] * res
    xn_ref[...] = xn
    hn_ref[...] = _modnorm(xn, g_ref[...], sh_ref[0], sc_ref[0])


def ab_out_pallas(oa, ob, w_out_b, x, gate, shift, scale, g):
    N, D = x.shape
    tm = PROJ_TM
    ms = _mod_spec(tm, N // gate.shape[0])
    return pl.pallas_call(
        _ab_out_kernel, grid=(N // tm,),
        in_specs=[_row_spec(tm, A_W), _row_spec(tm, QB_W), _full_spec((A_W + QB_W, D)), _row_spec(tm, D), ms, ms, ms,
                  _full_spec((1, D))],
        out_specs=[_row_spec(tm, D), _row_spec(tm, D)],
        out_shape=[jax.ShapeDtypeStruct((N, D), jnp.float32)] * 2,
        compiler_params=_cparams(1), name="ab_out",
    )(oa, ob, w_out_b, x, gate, shift, scale, g.reshape(1, D))


def _c_in_kernel(x_ref, sh_ref, sc_ref, g_ref, w_ref, vg_ref, u_ref, vn_ref):
    hb = _modnorm(x_ref[...], g_ref[...], sh_ref[0], sc_ref[0]).astype(jnp.bfloat16)
    res = _gelu_tanh(jnp.dot(hb, w_ref[...], preferred_element_type=jnp.float32))
    u_ref[...] = res[:, :C_WIDTH]
    vn_ref[...] = _rms(res[:, C_WIDTH:]) * vg_ref[...]


def c_in_pallas(x, shift, scale, g, w_in_b, vnorm_g):
    N, D = x.shape
    tm = PROJ_TM
    ms = _mod_spec(tm, N // shift.shape[0])
    return pl.pallas_call(
        _c_in_kernel, grid=(N // tm,),
        in_specs=[_row_spec(tm, D), ms, ms, _full_spec((1, D)), _full_spec((D, 2 * C_WIDTH)), _full_spec((1, C_WIDTH))],
        out_specs=[_row_spec(tm, C_WIDTH), _row_spec(tm, C_WIDTH)],
        out_shape=[jax.ShapeDtypeStruct((N, C_WIDTH), jnp.float32)] * 2,
        compiler_params=_cparams(1), name="c_in",
    )(x, shift, scale, g.reshape(1, D), w_in_b, vnorm_g.reshape(1, C_WIDTH))


def _c_mix_kernel(u_ref, vn_ref, ws_ref, bs_ref, w_ref, x_ref, gt_ref, sh_ref, sc_ref, g_ref, xn_ref, hn_ref, z_ref):
    tm = u_ref.shape[0]
    for cc in range(tm // C_CHUNK):
        rows = slice(cc * C_CHUNK, (cc + 1) * C_CHUNK)
        for grp in range(C_GROUPS):
            cols = slice(grp * C_GROUP_W, (grp + 1) * C_GROUP_W)
            mixed = _bdot(ws_ref[grp], vn_ref[rows, cols]) + bs_ref[:, grp:grp + 1]
            z_ref[rows, cols] = (u_ref[rows, cols] * mixed).astype(jnp.bfloat16)
    res = jnp.dot(z_ref[...], w_ref[...], preferred_element_type=jnp.float32)
    xn = x_ref[...] + gt_ref[0] * res
    xn_ref[...] = xn
    hn_ref[...] = _modnorm(xn, g_ref[...], sh_ref[0], sc_ref[0])


def c_mix_pallas(u, vn, w_s_b, bs, w_out_b, x, gate, shift, scale, g):
    N, D = x.shape
    tm = PROJ_TM
    ms = _mod_spec(tm, N // gate.shape[0])
    return pl.pallas_call(
        _c_mix_kernel, grid=(N // tm,),
        in_specs=[_row_spec(tm, C_WIDTH), _row_spec(tm, C_WIDTH), _full_spec((C_GROUPS, C_CHUNK, C_CHUNK)),
                  _full_spec((C_CHUNK, LANE)), _full_spec((C_WIDTH, D)), _row_spec(tm, D), ms, ms, ms, _full_spec((1, D))],
        out_specs=[_row_spec(tm, D), _row_spec(tm, D)],
        out_shape=[jax.ShapeDtypeStruct((N, D), jnp.float32)] * 2,
        scratch_shapes=[pltpu.VMEM((tm, C_WIDTH), jnp.bfloat16)],
        compiler_params=_cparams(1), name="c_mix",
    )(u, vn, w_s_b, bs, w_out_b, x, gate, shift, scale, g.reshape(1, D))


def _unit_tri_inverse_offdiag(a, ii, jj):
    def same_block(b):
        return jnp.right_shift(ii, b) == jnp.right_shift(jj, b)
    n = -jnp.where(same_block(1), a, 0.0)
    for b in range(1, 7):
        o = jnp.where(same_block(b + 1) & jnp.logical_not(same_block(b)), a, 0.0)
        p = o + _bdot(n, o)
        q = p + _bdot(p, n)
        n = n - q
    return n


DELTA_GROUP = 2


def _delta_kernel(q_ref, k_ref, v_ref, gate_ref, ab_ref, abt_ref, cw_ref, prow_ref, pcol_ref, og_ref, s0_ref,
                  o_ref, sfin_ref, u_s, w_s, qk_s, qd_s, ktt_s, egl_s, of_s, ob_s, *, use_s0):
    T = q_ref.shape[1]
    nck = T // DC
    h = pl.program_id(1)
    ii = lax.broadcasted_iota(jnp.int32, (DC, DC), 0)
    jj = lax.broadcasted_iota(jnp.int32, (DC, DC), 1)
    low = (jj <= ii).astype(jnp.float32)
    upp = (jj >= ii).astype(jnp.float32)
    r16 = lax.broadcasted_iota(jnp.int32, (AB_COLS, DC), 0)
    i2 = lax.broadcasted_iota(jnp.int32, (2 * DC, 2 * DC), 0)
    j2 = lax.broadcasted_iota(jnp.int32, (2 * DC, 2 * DC), 1)

    def shifted(ref, c, d):
        cur = ref[0, pl.ds(pl.multiple_of(c * DC, DC), DC), :]
        if d == 0:
            return cur
        step = 1 if d > 0 else -1
        cn = jnp.clip(c + step, 0, nck - 1)
        nb = ref[0, pl.ds(pl.multiple_of(cn * DC, DC), DC), :]
        ok = jnp.where((c + step >= 0) & (c + step < nck), 1.0, 0.0)
        rc = pltpu.roll(cur, (-d) % DC, axis=0)
        rn = pltpu.roll(nb, (-d) % DC, axis=0) * ok
        return jnp.where((ii + d >= 0) & (ii + d < DC), rc, rn)

    def prep_chunk(c):
        rows = pl.ds(pl.multiple_of(c * DC, DC), DC)

        def conv_silu(ref, idx):
            acc = jnp.zeros((DC, HEAD_DIM), jnp.float32)
            for kk in range(A_CONV):
                acc = acc + shifted(ref, c, kk - A_CONV // 2) * cw_ref[0, idx, kk:kk + 1, :]
            return acc * _sigmoid(acc)

        q = conv_silu(q_ref, 0)
        k = conv_silu(k_ref, 1)
        v = conv_silu(v_ref, 2)
        q = q * lax.rsqrt(jnp.sum(q * q, axis=-1, keepdims=True) + EPS) * (HEAD_DIM ** -0.5)
        k = k * lax.rsqrt(jnp.sum(k * k, axis=-1, keepdims=True) + EPS)
        kt = k.T
        ab = ab_ref[0, rows, :]
        g_all = -prow_ref[0:1, :] * _softplus(ab + prow_ref[1:2, :])
        beta_all = _sigmoid(ab)
        abt = abt_ref[:, rows]
        gr_all = -pcol_ref[0] * _softplus(abt + pcol_ref[1])

        def pick_col(x, col):
            return jnp.sum(jnp.where(jj == col, x, 0.0), axis=1, keepdims=True)

        def pick_row(x, row):
            return jnp.sum(jnp.where(r16 == row, x, 0.0), axis=0, keepdims=True)

        qkt = _bdot(q, kt)
        per_dir = []
        for dirn in range(2):
            col = dirn * A_HEADS + h
            gc = pick_col(_hdot(low if dirn == 0 else upp, g_all), col)
            gr = pick_row(_hdot(gr_all, upp if dirn == 0 else low), col)
            beta = pick_col(beta_all, 2 * A_HEADS + col)
            tri = (jj <= ii) if dirn == 0 else (jj >= ii)
            gamma = jnp.where(tri, jnp.exp(jnp.where(tri, gc - gr, 0.0)), 0.0)
            kb = k * beta
            egc = jnp.exp(gc)
            gl = gc[DC - 1:DC, :] if dirn == 0 else gc[0:1, :]
            qk_s[dirn, c] = qkt * gamma
            qd_s[dirn, c] = q * egc
            ktt_s[dirn, c] = kt * jnp.exp(gl - gr)
            egl_s[dirn, c] = jnp.broadcast_to(jnp.exp(gl), (8, DC))
            per_dir.append((kb, gamma, jnp.concatenate([v * beta, kb * egc], axis=1)))
        kk = _bdot(jnp.concatenate([per_dir[0][0], per_dir[1][0]], axis=0), kt)
        zero = jnp.zeros((DC, DC), jnp.float32)
        a_f = jnp.where(jj < ii, kk[:DC] * per_dir[0][1], 0.0)
        a_b = jnp.where(jj > ii, kk[DC:] * per_dir[1][1], 0.0)
        a = jnp.concatenate([jnp.concatenate([a_f, zero], axis=1), jnp.concatenate([zero, a_b], axis=1)], axis=0)
        n = _unit_tri_inverse_offdiag(a, i2, j2)
        rhs = jnp.concatenate([per_dir[0][2], per_dir[1][2]], axis=0)
        sol = rhs + _bdot(n, rhs)
        for dirn in range(2):
            u_s[dirn, c] = sol[dirn * DC:(dirn + 1) * DC, :HEAD_DIM]
            w_s[dirn, c] = sol[dirn * DC:(dirn + 1) * DC, HEAD_DIM:]

    def prep(cg, carry):
        for g in range(DELTA_GROUP):
            prep_chunk(cg * DELTA_GROUP + g)
        return carry

    lax.fori_loop(0, nck // DELTA_GROUP, prep, 0)

    def chunk_update(S, c, dirn):
        v_new = u_s[dirn, c] - _bdot(w_s[dirn, c], S)
        o = _bdot(qd_s[dirn, c], S) + _bdot(qk_s[dirn, c], v_new)
        S = S * egl_s[dirn, c, 0:1, :] + _bdot(ktt_s[dirn, c], v_new)
        return S, o

    def step(i, carry):
        sf, sb = carry
        sf, o_f = chunk_update(sf, i, 0)
        of_s[pl.ds(pl.multiple_of(i * DC, DC), DC), :] = o_f
        cb = nck - 1 - i
        sb, o_b = chunk_update(sb, cb, 1)
        ob_s[pl.ds(pl.multiple_of(cb * DC, DC), DC), :] = o_b
        return sf, sb

    if use_s0:
        init = (s0_ref[0, 0, 0], s0_ref[0, 1, 0])
    else:
        init = (jnp.zeros((HEAD_DIM, HEAD_DIM), jnp.float32), jnp.zeros((HEAD_DIM, HEAD_DIM), jnp.float32))
    sf, sb = lax.fori_loop(0, nck, step, init)
    sfin_ref[0, 0, 0] = sf
    sfin_ref[0, 1, 0] = sb

    def fin(c, carry):
        rows = pl.ds(pl.multiple_of(c * DC, DC), DC)
        o = of_s[rows, :] + ob_s[rows, :]
        o = _rms(o) * og_ref[...]
        g = gate_ref[0, rows, :]
        o_ref[0, rows, :] = o * (g * _sigmoid(g))
        return carry

    lax.fori_loop(0, nck, fin, 0)


def delta_mixer_pallas(qkv, gate, ab, abt, cw, prow, pcol, onorm_g, s0):
    B, T, _ = qkv.shape
    H = A_HEADS
    state_block = (1, 2, 1, HEAD_DIM, HEAD_DIM)
    use_s0 = s0 is not None
    if use_s0:
        s0_spec = pl.BlockSpec(state_block, lambda b, h: (b, 0, h, 0, 0))
    else:
        s0 = jnp.zeros(state_block, jnp.float32)
        s0_spec = _full_spec(state_block)
    nck = T // DC
    seq = lambda off: pl.BlockSpec((1, T, HEAD_DIM), lambda b, h: (b, 0, off + h))
    assert nck % DELTA_GROUP == 0
    tok = pltpu.VMEM((T, HEAD_DIM), jnp.float32)
    blk = pltpu.VMEM((2, nck, DC, HEAD_DIM), jnp.float32)
    return pl.pallas_call(
        functools.partial(_delta_kernel, use_s0=use_s0),
        grid=(B, H),
        in_specs=[seq(0), seq(H), seq(2 * H), seq(0),
                  pl.BlockSpec((1, T, LANE), lambda b, h: (b, 0, 0)), pl.BlockSpec((AB_COLS, T), lambda b, h: (0, b)),
                  pl.BlockSpec((1, 3, A_CONV, HEAD_DIM), lambda b, h: (h, 0, 0, 0)),
                  _full_spec((2, LANE)), _full_spec((2, AB_COLS, LANE)), _full_spec((1, HEAD_DIM)), s0_spec],
        out_specs=[seq(0), pl.BlockSpec(state_block, lambda b, h: (b, 0, h, 0, 0))],
        out_shape=[jax.ShapeDtypeStruct((B, T, A_W), jnp.float32),
                   jax.ShapeDtypeStruct((B, 2, H, HEAD_DIM, HEAD_DIM), jnp.float32)],
        scratch_shapes=[blk, blk, blk, blk, blk, pltpu.VMEM((2, nck, 8, DC), jnp.float32), tok, tok],
        compiler_params=_cparams(2), name="delta",
    )(qkv, qkv, qkv, gate, ab, abt, cw, prow, pcol, onorm_g, s0)


def _softmax_segments(scores, values, sink):
    m = jnp.full((scores[0].shape[0], 1), sink, jnp.float32)
    for s in scores:
        m = jnp.maximum(m, jnp.max(s, axis=-1, keepdims=True))
    den = jnp.exp(sink - m)
    o = None
    for s, v in zip(scores, values):
        p = jnp.exp(s - m)
        den = den + jnp.sum(p, axis=-1, keepdims=True)
        pv = _bdot(p, v)
        o = pv if o is None else o + pv
    return o / den


def _ctx_attn_kernel(sink_ref, q_ref, k_ref, v_ref, o_ref):
    hkv = pl.program_id(1)
    k = k_ref[0]
    v = v_ref[0]
    for g in range(B_GROUP):
        q = q_ref[0, :, g * HEAD_DIM:(g + 1) * HEAD_DIM]
        s = _bdot_nt(q, k) * (HEAD_DIM ** -0.5)
        o_ref[0, :, g * HEAD_DIM:(g + 1) * HEAD_DIM] = _softmax_segments([s], [v], sink_ref[hkv * B_GROUP + g])


def context_attention_pallas(qb, kb, vb, sink):
    B, S, _ = qb.shape
    gw = B_GROUP * HEAD_DIM
    return pl.pallas_call(
        _ctx_attn_kernel,
        grid=(B, B_KV_HEADS),
        in_specs=[pl.BlockSpec(memory_space=pltpu.SMEM),
                  pl.BlockSpec((1, S, gw), lambda b, h: (b, 0, h)),
                  pl.BlockSpec((1, S, HEAD_DIM), lambda b, h: (b, 0, h)),
                  pl.BlockSpec((1, S, HEAD_DIM), lambda b, h: (b, 0, h))],
        out_specs=pl.BlockSpec((1, S, gw), lambda b, h: (b, 0, h)),
        out_shape=jax.ShapeDtypeStruct(qb.shape, jnp.float32),
        compiler_params=_cparams(2), name="ctx_attn",
    )(sink, qb, kb, vb)


def rope_tables(n_tokens):
    t = np.arange((n_tokens // GRID_W) * GRID_W)
    row = jnp.asarray((t // GRID_W).astype(np.float32))
    col = jnp.asarray((t % GRID_W).astype(np.float32))
    nf = HEAD_DIM // 4
    inv = ROPE_BASE ** (-jnp.arange(nf, dtype=jnp.float32) / nf)
    ar = row[:, None] * inv
    ac = col[:, None] * inv
    rc = jnp.concatenate([jnp.cos(ar), jnp.cos(ar), jnp.cos(ac), jnp.cos(ac)], axis=-1)
    rs = jnp.concatenate([-jnp.sin(ar), jnp.sin(ar), -jnp.sin(ac), jnp.sin(ac)], axis=-1)
    return rc, rs


def _rope(x, rc, rs):
    lane = lax.broadcasted_iota(jnp.int32, x.shape, 1)
    nf = HEAD_DIM // 4
    swapped = jnp.where((lane & (2 * nf - 1)) < nf, pltpu.roll(x, HEAD_DIM - nf, axis=1), pltpu.roll(x, nf, axis=1))
    return x * rc + swapped * rs


def _lat_attn_kernel(sink_ref, q_ref, kp_ref, kc_ref, kn_ref, vp_ref, vc_ref, vn_ref, kx_ref, vx_ref,
                     rcp_ref, rcc_ref, rcn_ref, rsp_ref, rsc_ref, rsn_ref, o_ref):
    hkv = pl.program_id(1)
    i = pl.program_id(2)
    nb = pl.num_programs(2)
    r = lax.broadcasted_iota(jnp.int32, (ATTN_BLOCK, ATTN_BLOCK), 0)
    c = lax.broadcasted_iota(jnp.int32, (ATTN_BLOCK, ATTN_BLOCK), 1)
    kp = _rope(kp_ref[0], rcp_ref[...], rsp_ref[...])
    kc = _rope(kc_ref[0], rcc_ref[...], rsc_ref[...])
    kn = _rope(kn_ref[0], rcn_ref[...], rsn_ref[...])
    ok_p = (c >= r) & (i > 0)
    ok_n = (c <= r) & (i < nb - 1)
    scale = HEAD_DIM ** -0.5
    for g in range(B_GROUP):
        q = _rope(q_ref[0, :, g * HEAD_DIM:(g + 1) * HEAD_DIM], rcc_ref[...], rsc_ref[...])
        sx = _bdot_nt(q, kx_ref[0]) * scale
        sp = jnp.where(ok_p, _bdot_nt(q, kp) * scale, NEG_INF)
        sc = _bdot_nt(q, kc) * scale
        sn = jnp.where(ok_n, _bdot_nt(q, kn) * scale, NEG_INF)
        o_ref[0, :, g * HEAD_DIM:(g + 1) * HEAD_DIM] = _softmax_segments(
            [sx, sp, sc, sn], [vx_ref[0], vp_ref[0], vc_ref[0], vn_ref[0]], sink_ref[hkv * B_GROUP + g])


def latent_attention_pallas(qb, kb, vb, k_ctx, v_ctx, sink):
    assert WINDOW == ATTN_BLOCK
    B, T, _ = qb.shape
    P = k_ctx.shape[1]
    nb = T // ATTN_BLOCK
    gw = B_GROUP * HEAD_DIM
    rc, rs = rope_tables(T)
    prev = lambda b, h, i: (b, jnp.maximum(i - 1, 0), h)
    cur = lambda b, h, i: (b, i, h)
    nxt = lambda b, h, i: (b, jnp.minimum(i + 1, nb - 1), h)
    blk = lambda f: pl.BlockSpec((1, ATTN_BLOCK, HEAD_DIM), f)
    tab = lambda f: pl.BlockSpec((ATTN_BLOCK, HEAD_DIM), lambda b, h, i: (f(b, h, i)[1], 0))
    ctx = pl.BlockSpec((1, P, HEAD_DIM), lambda b, h, i: (b, 0, h))
    return pl.pallas_call(
        _lat_attn_kernel,
        grid=(B, B_KV_HEADS, nb),
        in_specs=[pl.BlockSpec(memory_space=pltpu.SMEM), pl.BlockSpec((1, ATTN_BLOCK, gw), cur),
                  blk(prev), blk(cur), blk(nxt), blk(prev), blk(cur), blk(nxt), ctx, ctx,
                  tab(prev), tab(cur), tab(nxt), tab(prev), tab(cur), tab(nxt)],
        out_specs=pl.BlockSpec((1, ATTN_BLOCK, gw), cur),
        out_shape=jax.ShapeDtypeStruct(qb.shape, jnp.float32),
        compiler_params=_cparams(3), name="lat_attn",
    )(sink, qb, kb, kb, kb, vb, vb, vb, k_ctx, v_ctx, rc, rc, rc, rs, rs, rs)


def _top16(s, iota, iota16, want_rank):
    cur = s
    rank = jnp.full(s.shape, float(PEER_TOPK), jnp.float32) if want_rank else None
    sv = jnp.zeros((PEER_TOPK, s.shape[1]), jnp.float32)
    si = jnp.zeros((PEER_TOPK, s.shape[1]), jnp.float32)
    for k in range(PEER_TOPK):
        m = jnp.max(cur, axis=0, keepdims=True)
        idx = jnp.min(jnp.where(cur == m, iota, float(N_KEYS)), axis=0, keepdims=True)
        sel = iota == idx
        if want_rank:
            rank = jnp.where(sel, float(k), rank)
        else:
            si = jnp.where(iota16 == float(k), idx, si)
        cur = jnp.where(sel, -jnp.inf, cur)
        sv = jnp.where(iota16 == float(k), m, sv)
    return sv, (rank if want_rank else si)


def _top16_distinct(s, iota16):
    cur = s
    rank = jnp.full(s.shape, float(PEER_TOPK), jnp.float32)
    sv = jnp.zeros((PEER_TOPK, s.shape[1]), jnp.float32)
    for k in range(PEER_TOPK):
        m = jnp.max(cur, axis=0, keepdims=True)
        sel = cur == m
        rank = jnp.where(sel, float(k), rank)
        cur = jnp.where(sel, -jnp.inf, cur)
        sv = jnp.where(iota16 == float(k), m, sv)
    return sv, rank


def _staircase(sv0, sv1, iota16):
    n = jnp.zeros_like(sv0)
    f = sv0 + sv1[0:1, :]
    top = f[0:1, :]
    z = jnp.zeros_like(top)
    for _ in range(PEER_TOPK):
        fm = jnp.max(f, axis=0, keepdims=True)
        cand = iota16 * float(PEER_TOPK) + n
        ci = jnp.min(jnp.where(f == fm, cand, 1e9), axis=0, keepdims=True)
        win = cand == ci
        z = z + jnp.exp(fm - top)
        n = jnp.where(win, n + 1.0, n)
        nw = jnp.max(jnp.where(win, n, 0.0), axis=0, keepdims=True)
        nxt = jnp.max(jnp.where(iota16 == nw, sv1, -jnp.inf), axis=0, keepdims=True)
        f = jnp.where(win, sv0 + nxt, f)
    return n, z


def _peer_route_kernel(h_ref, wq_ref, keys_ref, ht_ref, r1_ref, p1_ref, c0_ref, q0_ref, q_scr, s_scr):
    tm = h_ref.shape[0]
    hf = h_ref[...]
    ht_ref[...] = hf.T.astype(jnp.bfloat16)
    q_scr[...] = jnp.dot(hf.astype(jnp.bfloat16), wq_ref[...], preferred_element_type=jnp.float32)

    def head_body(h, carry):
        for p in range(2):
            off = pl.multiple_of(h * D_KEY + p * (D_KEY // 2), LANE)
            qhp = q_scr[:, pl.ds(off, D_KEY // 2)]
            s_scr[p] = lax.dot_general(keys_ref[p], qhp, (((1,), (1,)), ((), ())), precision=lax.Precision.HIGHEST,
                                       preferred_element_type=jnp.float32)

        iota = lax.broadcasted_iota(jnp.int32, (N_KEYS, LANE), 0).astype(jnp.float32)
        iota16 = lax.broadcasted_iota(jnp.int32, (PEER_TOPK, LANE), 0).astype(jnp.float32)

        def emit(lanes, s0, s1, sv0, sv1, rank1, c0_terms):
            n, z = _staircase(sv0, sv1, iota16)
            c0 = jnp.zeros_like(s0)
            for k in range(PEER_TOPK):
                c0 = jnp.where(c0_terms(k), n[k:k + 1, :], c0)
            p1 = jnp.exp(s1 - sv1[0:1, :])
            r1_ref[h, :, lanes] = pltpu.bitcast(rank1.astype(jnp.bfloat16), jnp.uint32)
            p1_ref[h, :, lanes] = pltpu.bitcast(p1.astype(jnp.bfloat16), jnp.uint32)
            c0_ref[h, :, lanes] = c0
            q0_ref[h, :, lanes] = jnp.exp(s0 - sv0[0:1, :]) / z

        picked = None
        for c in range(tm // LANE):
            lanes = slice(c * LANE, (c + 1) * LANE)
            s0 = s_scr[0, :, lanes]
            s1 = s_scr[1, :, lanes]
            sv0, rank0 = _top16_distinct(s0, iota16)
            sv1, rank1 = _top16_distinct(s1, iota16)
            emit(lanes, s0, s1, sv0, sv1, rank1, lambda k: rank0 == float(k))
            cnt = (jnp.sum(jnp.where(rank0 < float(PEER_TOPK), 1.0, 0.0), axis=0, keepdims=True)
                   + jnp.sum(jnp.where(rank1 < float(PEER_TOPK), 1.0, 0.0), axis=0, keepdims=True))
            picked = cnt if picked is None else jnp.maximum(picked, cnt)

        @pl.when(jnp.max(picked) > float(2 * PEER_TOPK))
        def _():
            def exact_chunk(c, carry2):
                lanes = pl.ds(pl.multiple_of(c * LANE, LANE), LANE)
                s0 = s_scr[0, :, lanes]
                s1 = s_scr[1, :, lanes]
                sv0, si0 = _top16(s0, iota, iota16, False)
                sv1, rank1 = _top16(s1, iota, iota16, True)
                emit(lanes, s0, s1, sv0, sv1, rank1, lambda k: iota == si0[k:k + 1, :])
                return carry2

            lax.fori_loop(0, tm // LANE, exact_chunk, 0)

        return carry

    lax.fori_loop(0, PEER_HEADS, head_body, 0)


def _peer_expert_kernel(ht_ref, r1_ref, p1_ref, c0_ref, q0_ref, ed_ref, eut_ref, x_ref, gt_ref, fg_ref, out_ref,
                        acc_ref, act_ref, wg_ref, *, final_norm):
    e = pl.program_id(1)
    tm = ht_ref.shape[1]
    te = ed_ref.shape[0]
    bf16 = jnp.bfloat16

    @pl.when(e == 0)
    def _():
        acc_ref[...] = jnp.zeros_like(acc_ref)

    for sb in range(te // EXPERT_SUB):
        rs = slice(sb * EXPERT_SUB, (sb + 1) * EXPERT_SUB)
        act_ref[rs, :] = jnp.dot(ed_ref[rs, :], ht_ref[...], preferred_element_type=jnp.float32)
    half = N_KEYS // 2
    for ii in range(te // N_KEYS):
        for c in range(tm // LANE):
            lanes = slice(c * LANE, (c + 1) * LANE)
            g = jnp.zeros((N_KEYS, LANE), bf16)
            for h in range(PEER_HEADS):
                c0b = jnp.broadcast_to(c0_ref[h, ii:ii + 1, lanes], (N_KEYS, LANE)).astype(bf16)
                q0b = jnp.broadcast_to(q0_ref[h, ii:ii + 1, lanes], (N_KEYS, LANE)).astype(bf16)
                r1b = pltpu.bitcast(r1_ref[h, :, lanes], bf16)
                p1b = pltpu.bitcast(p1_ref[h, :, lanes], bf16)
                g = g + jnp.where(r1b < c0b, p1b, jnp.zeros((), bf16)) * q0b
            ab = _gelu_tanh(act_ref[ii * N_KEYS:(ii + 1) * N_KEYS, lanes].astype(bf16))
            wg_ref[ii * half:(ii + 1) * half, lanes] = pltpu.bitcast(g * ab, jnp.uint32)
    for ub in range(te // EXPERT_UP):
        acc_ref[...] += jnp.dot(eut_ref[:, ub * EXPERT_UP:(ub + 1) * EXPERT_UP],
                                pltpu.bitcast(wg_ref[ub * EXPERT_UP // 2:(ub + 1) * EXPERT_UP // 2, :], bf16),
                                preferred_element_type=jnp.float32)

    @pl.when(e == pl.num_programs(1) - 1)
    def _():
        y = x_ref[...] + gt_ref[0] * acc_ref[...].T
        if final_norm:
            y = _rms(y) * fg_ref[...]
        out_ref[...] = y


def peer_pallas(h, x, gate, wq_b, keys, ed_b, eut_b, final_g, final_norm, layer):
    T, D = h.shape
    nq = PEER_HEADS * D_KEY
    route_shape = jax.ShapeDtypeStruct((PEER_HEADS, N_KEYS, T), jnp.float32)
    route_shape_b = jax.ShapeDtypeStruct((PEER_HEADS, N_KEYS // 2, T), jnp.uint32)
    tmr = ROUTE_TM
    route_spec = pl.BlockSpec((PEER_HEADS, N_KEYS, tmr), lambda i: (0, 0, i))
    route_spec_b = pl.BlockSpec((PEER_HEADS, N_KEYS // 2, tmr), lambda i: (0, 0, i))
    ht, r1, p1, c0, q0 = pl.pallas_call(
        _peer_route_kernel,
        grid=(T // tmr,),
        in_specs=[_row_spec(tmr, D), pl.BlockSpec((None, D, nq), lambda i: (layer, 0, 0)),
                  pl.BlockSpec((None, 2, N_KEYS, D_KEY // 2), lambda i: (layer, 0, 0, 0))],
        out_specs=[pl.BlockSpec((D, tmr), lambda i: (0, i)), route_spec_b, route_spec_b, route_spec, route_spec],
        out_shape=[jax.ShapeDtypeStruct((D, T), jnp.bfloat16), route_shape_b, route_shape_b, route_shape, route_shape],
        scratch_shapes=[pltpu.VMEM((tmr, nq), jnp.float32), pltpu.VMEM((2, N_KEYS, tmr), jnp.float32)],
        compiler_params=_cparams(1), name="peer_route",
    )(h, wq_b, keys)

    tm, te = EXPERT_TM, EXPERT_TE
    packed_spec = pl.BlockSpec((PEER_HEADS, N_KEYS // 2, tm), lambda i, e: (0, 0, i))
    row_spec = pl.BlockSpec((PEER_HEADS, te // N_KEYS, tm), lambda i, e: (0, e, i))
    return pl.pallas_call(
        functools.partial(_peer_expert_kernel, final_norm=final_norm),
        grid=(T // tm, N_EXPERTS // te),
        in_specs=[pl.BlockSpec((D, tm), lambda i, e: (0, i)), packed_spec, packed_spec, row_spec, row_spec,
                  pl.BlockSpec((None, te, D), lambda i, e: (layer, e, 0)),
                  pl.BlockSpec((None, D, te), lambda i, e: (layer, 0, e)),
                  pl.BlockSpec((tm, D), lambda i, e: (i, 0)), _mod_spec(tm, T // gate.shape[0]), _full_spec((1, D))],
        out_specs=pl.BlockSpec((tm, D), lambda i, e: (i, 0)),
        out_shape=jax.ShapeDtypeStruct((T, D), jnp.float32),
        scratch_shapes=[pltpu.VMEM((D, tm), jnp.float32), pltpu.VMEM((te, tm), jnp.float32), pltpu.VMEM((te // 2, tm), jnp.uint32)],
        compiler_params=_cparams(2), name="peer_expert",
    )(ht, r1, p1, c0, q0, ed_b, eut_b, x, gate, final_g.reshape(1, D))


def kernel(x_prompt, x_sample, cache_attn_k, cache_attn_v, state_delta, c, c_ctx, w_mod, b_mod, norm_g, w_in_ab, conv_ab, a_log_ab, dt_bias_ab, onorm_ab, sink_ab, w_out_ab, w_in_c, vnorm_c, w_s_c, b_s_c, w_out_c, peer_wq, peer_keys, peer_down, peer_up, final_norm_g):
    f32, bf16 = jnp.float32, jnp.bfloat16
    BP, SP, D = x_prompt.shape
    BS, TS, _ = x_sample.shape
    past = cache_attn_k.shape[2]
    H = A_HEADS
    cvecs = jnp.zeros((8, D), f32).at[0].set(c_ctx).at[1:1 + BS].set(c)
    streams = [dict(x=x_prompt.reshape(BP * SP, D), B=BP, T=SP, rows=slice(0, 1)),
               dict(x=x_sample.reshape(BS * TS, D), B=BS, T=TS, rows=slice(1, 1 + BS))]
    new_k = new_v = new_s = None
    wq_b = peer_wq.astype(bf16)
    ed_b = peer_down.astype(bf16)
    eut_b = jnp.swapaxes(peer_up, 1, 2).astype(bf16)
    for layer in range(DEPTH):
        m = modulation_pallas(cvecs, w_mod[layer], b_mod[layer])
        li = layer // 2
        last = layer == DEPTH - 1
        if layer % 2 == 0:
            w_in = w_in_ab[li]
            c0 = QKV_W + A_W
            w_main = jnp.concatenate([w_in[:, :c0], w_in[:, c0 + AB_COLS:]], axis=1).astype(bf16)
            w_log = w_in[:, c0:c0 + AB_COLS].astype(bf16)
            w_log_pad = jnp.pad(w_log, ((0, 0), (0, LANE - AB_COLS)))
            w_out_b = w_out_ab[li].astype(bf16)
            cw = conv_ab[li].reshape(A_CONV, 3, H, HEAD_DIM).transpose(2, 1, 0, 3)
            a_dec = jnp.exp(a_log_ab[li].astype(f32)).reshape(2 * H)
            dt = dt_bias_ab[li].astype(f32).reshape(2 * H)
            prow = jnp.zeros((2, LANE), f32).at[0, :2 * H].set(a_dec).at[1, :2 * H].set(dt)
            pcol = jnp.zeros((2, AB_COLS, LANE), f32).at[0, :2 * H].set(a_dec[:, None]).at[1, :2 * H].set(dt[:, None])
            og = onorm_ab[li].reshape(1, HEAD_DIM).astype(f32)
            sink = sink_ab[li].astype(f32)
        else:
            w_in_b = w_in_c[li].astype(bf16)
            w_s_b = w_s_c[li].astype(bf16)
            bs = jnp.pad(b_s_c[li].T.astype(f32), ((0, 0), (0, LANE - C_GROUPS)))
            w_out_b = w_out_c[li].astype(bf16)
        for si, st in enumerate(streams):
            B, T = st["B"], st["T"]
            mk = [m[st["rows"], k * D:(k + 1) * D].reshape(-1, 1, D) for k in range(6)]
            x = st["x"]
            if layer % 2 == 0:
                qkv, gate, qb, kb, vb, ab, abt = ab_in_pallas(x, mk[0], mk[1], norm_g[layer, 0], w_main, w_log_pad, w_log.T)
                s0 = None if si == 0 else state_delta[:, li]
                o_a, s_fin = delta_mixer_pallas(qkv.reshape(B, T, QKV_W), gate.reshape(B, T, A_W), ab.reshape(B, T, LANE),
                                                abt, cw, prow, pcol, og, s0)
                if si == 0:
                    o_b = context_attention_pallas(qb.reshape(B, T, QB_W), kb.reshape(B, T, KV_W), vb.reshape(B, T, KV_W), sink)
                    new_k = kb.reshape(B, 1, T, B_KV_HEADS, HEAD_DIM)
                    new_v = vb.reshape(B, 1, T, B_KV_HEADS, HEAD_DIM)
                    new_s = s_fin[:, None]
                else:
                    o_b = latent_attention_pallas(qb.reshape(B, T, QB_W), kb.reshape(B, T, KV_W), vb.reshape(B, T, KV_W),
                                                  cache_attn_k[:, li].reshape(B, past, KV_W),
                                                  cache_attn_v[:, li].reshape(B, past, KV_W), sink)
                x, hn = ab_out_pallas(o_a.reshape(B * T, A_W), o_b.reshape(B * T, QB_W), w_out_b, x, mk[2], mk[3], mk[4],
                                      norm_g[layer, 1])
            else:
                u, vn = c_in_pallas(x, mk[0], mk[1], norm_g[layer, 0], w_in_b, vnorm_c[li])
                x, hn = c_mix_pallas(u, vn, w_s_b, bs, w_out_b, x, mk[2], mk[3], mk[4], norm_g[layer, 1])
            st["x"] = peer_pallas(hn, x, mk[5], wq_b, peer_keys, ed_b, eut_b, final_norm_g, last, layer)
    y_prompt = streams[0]["x"].reshape(BP, SP, D)
    y_sample = streams[1]["x"].reshape(BS, TS, D)
    return (y_prompt, y_sample, new_k, new_v, new_s)
```

```python
import functools
import math

import numpy as np
import jax
import jax.numpy as jnp
from jax import lax
from jax.experimental import pallas as pl
from jax.experimental.pallas import tpu as pltpu

D_MODEL = 1024
DEPTH = 2
GRID_W = 64
HEAD_DIM = 128
EPS = 1e-6
NEG_INF = -1e30
A_HEADS = D_MODEL // (2 * HEAD_DIM)
A_W = A_HEADS * HEAD_DIM
A_CONV = 5
B_HEADS = D_MODEL // (2 * HEAD_DIM)
B_KV_HEADS = B_HEADS // 2
B_GROUP = B_HEADS // B_KV_HEADS
WINDOW = 128
ATTN_BLOCK = 128
ROPE_BASE = 10000.0
C_WIDTH = D_MODEL
C_GROUPS = 8
C_GROUP_W = C_WIDTH // C_GROUPS
C_CHUNK = 128
PEER_HEADS = 8
N_KEYS = 128
N_EXPERTS = N_KEYS * N_KEYS
PEER_TOPK = 16
D_KEY = 256

LANE = 128
VMEM_LIMIT = 48 * 1024 * 1024
PROJ_TM = 512
ROUTE_TM = 512
EXPERT_TM = 512
EXPERT_TE = 2048
EXPERT_SUB = 256
EXPERT_UP = 1024
MOD_TN = 1536
DC = 128

QKV_W = 3 * A_W
QB_W = B_HEADS * HEAD_DIM
KV_W = B_KV_HEADS * HEAD_DIM
AB_COLS = 4 * A_HEADS
AB_MAIN = QKV_W + A_W + QB_W + 2 * KV_W


def _bdot(a, b):
    return jnp.dot(a.astype(jnp.bfloat16), b.astype(jnp.bfloat16), preferred_element_type=jnp.float32)


def _bdot_nt(a, b):
    return lax.dot_general(a.astype(jnp.bfloat16), b.astype(jnp.bfloat16), (((1,), (1,)), ((), ())),
                           preferred_element_type=jnp.float32)


def _sigmoid(x):
    return 1.0 / (1.0 + jnp.exp(-x))


def _softplus(x):
    return jnp.maximum(x, 0.0) + jnp.log(1.0 + jnp.exp(-jnp.abs(x)))


def _gelu_tanh(x):
    k0 = -2.0 * math.sqrt(2.0 / math.pi) * math.log2(math.e)
    return x / (1.0 + jnp.exp2(x * (k0 + (k0 * 0.044715) * (x * x))))


def _rms(x):
    return x * lax.rsqrt(jnp.mean(x * x, axis=-1, keepdims=True) + EPS)


def _modnorm(x, g, shift, scale):
    return _rms(x) * g * (1.0 + scale) + shift


def _cparams(n):
    return pltpu.CompilerParams(dimension_semantics=("arbitrary",) * n, vmem_limit_bytes=VMEM_LIMIT)


def _mod_spec(tm, tokens_per_mod):
    return pl.BlockSpec((1, 1, D_MODEL), lambda i, *_: ((i * tm) // tokens_per_mod, 0, 0))


def _row_spec(tm, width):
    return pl.BlockSpec((tm, width), lambda i: (i, 0))


def _full_spec(shape):
    return pl.BlockSpec(shape, lambda *_: (0,) * len(shape))


def _mod_kernel(c_ref, w_ref, b_ref, o_ref):
    c = c_ref[...]
    o_ref[...] = _bdot(c * _sigmoid(c), w_ref[...]) + b_ref[...]


def modulation_pallas(cvecs, w_mod, b_mod):
    R, D = cvecs.shape
    N = w_mod.shape[1]
    return pl.pallas_call(
        _mod_kernel, grid=(N // MOD_TN,),
        in_specs=[pl.BlockSpec((R, D), lambda j: (0, 0)), pl.BlockSpec((D, MOD_TN), lambda j: (0, j)),
                  pl.BlockSpec((1, MOD_TN), lambda j: (0, j))],
        out_specs=pl.BlockSpec((R, MOD_TN), lambda j: (0, j)),
        out_shape=jax.ShapeDtypeStruct((R, N), jnp.float32),
        compiler_params=_cparams(1), name="modulation",
    )(cvecs, w_mod, b_mod.reshape(1, N))


def _ab_in_kernel(x_ref, sh_ref, sc_ref, g_ref, w_ref, wl_ref, wlt_ref,
                  qkv_ref, gate_ref, qb_ref, kb_ref, vb_ref, ab_ref, abt_ref):
    hb = _modnorm(x_ref[...], g_ref[...], sh_ref[0], sc_ref[0]).astype(jnp.bfloat16)
    res = jnp.dot(hb, w_ref[...], preferred_element_type=jnp.float32)
    o = 0
    for ref, wd in ((qkv_ref, QKV_W), (gate_ref, A_W), (qb_ref, QB_W), (kb_ref, KV_W), (vb_ref, KV_W)):
        ref[...] = res[:, o:o + wd]
        o += wd
    ab_ref[...] = jnp.dot(hb, wl_ref[...], preferred_element_type=jnp.float32)
    abt_ref[...] = lax.dot_general(wlt_ref[...], hb, (((1,), (1,)), ((), ())), preferred_element_type=jnp.float32)


def ab_in_pallas(x, shift, scale, g, w_main, w_log_pad, w_log_t):
    N, D = x.shape
    tm = PROJ_TM
    ms = _mod_spec(tm, N // shift.shape[0])
    outs = [(N, QKV_W), (N, A_W), (N, QB_W), (N, KV_W), (N, KV_W), (N, LANE)]
    return pl.pallas_call(
        _ab_in_kernel, grid=(N // tm,),
        in_specs=[_row_spec(tm, D), ms, ms, _full_spec((1, D)), _full_spec((D, AB_MAIN)), _full_spec((D, LANE)),
                  _full_spec((AB_COLS, D))],
        out_specs=[_row_spec(tm, s[1]) for s in outs] + [pl.BlockSpec((AB_COLS, tm), lambda i: (0, i))],
        out_shape=[jax.ShapeDtypeStruct(s, jnp.float32) for s in outs] + [jax.ShapeDtypeStruct((AB_COLS, N), jnp.float32)],
        compiler_params=_cparams(1), name="ab_in",
    )(x, shift, scale, g.reshape(1, D), w_main, w_log_pad, w_log_t)


def _ab_out_kernel(oa_ref, ob_ref, w_ref, x_ref, gt_ref, sh_ref, sc_ref, g_ref, xn_ref, hn_ref):
    res = _bdot(oa_ref[...], w_ref[0:A_W, :]) + _bdot(ob_ref[...], w_ref[A_W:, :])
    xn = x_ref[...] + gt_ref[0] * res
    xn_ref[...] = xn
    hn_ref[...] = _modnorm(xn, g_ref[...], sh_ref[0], sc_ref[0])


def ab_out_pallas(oa, ob, w_out_b, x, gate, shift, scale, g):
    N, D = x.shape
    tm = PROJ_TM
    ms = _mod_spec(tm, N // gate.shape[0])
    return pl.pallas_call(
        _ab_out_kernel, grid=(N // tm,),
        in_specs=[_row_spec(tm, A_W), _row_spec(tm, QB_W), _full_spec((A_W + QB_W, D)), _row_spec(tm, D), ms, ms, ms,
                  _full_spec((1, D))],
        out_specs=[_row_spec(tm, D), _row_spec(tm, D)],
        out_shape=[jax.ShapeDtypeStruct((N, D), jnp.float32)] * 2,
        compiler_params=_cparams(1), name="ab_out",
    )(oa, ob, w_out_b, x, gate, shift, scale, g.reshape(1, D))


def _c_layer_kernel(x_ref, sh1_ref, sc1_ref, g1_ref, wi_ref, vg_ref, ws_ref, bs_ref, wo_ref, gt_ref, sh2_ref, sc2_ref,
                    g2_ref, xn_ref, hn_ref, u_ref, vn_ref, z_ref):
    tm = x_ref.shape[0]
    x = x_ref[...]
    hb = _modnorm(x, g1_ref[...], sh1_ref[0], sc1_ref[0]).astype(jnp.bfloat16)
    res = _gelu_tanh(jnp.dot(hb, wi_ref[...], preferred_element_type=jnp.float32))
    u_ref[...] = res[:, :C_WIDTH]
    vn_ref[...] = _rms(res[:, C_WIDTH:]) * vg_ref[...]
    for cc in range(tm // C_CHUNK):
        rows = slice(cc * C_CHUNK, (cc + 1) * C_CHUNK)
        for grp in range(C_GROUPS):
            cols = slice(grp * C_GROUP_W, (grp + 1) * C_GROUP_W)
            mixed = _bdot(ws_ref[grp], vn_ref[rows, cols]) + bs_ref[:, grp:grp + 1]
            z_ref[rows, cols] = (u_ref[rows, cols] * mixed).astype(jnp.bfloat16)
    out = jnp.dot(z_ref[...], wo_ref[...], preferred_element_type=jnp.float32)
    xn = x + gt_ref[0] * out
    xn_ref[...] = xn
    hn_ref[...] = _modnorm(xn, g2_ref[...], sh2_ref[0], sc2_ref[0])


def c_layer_pallas(x, mk, g1, g2, w_in_b, vnorm_g, w_s_b, bs, w_out_b):
    N, D = x.shape
    tm = PROJ_TM
    ms = _mod_spec(tm, N // mk[0].shape[0])
    return pl.pallas_call(
        _c_layer_kernel, grid=(N // tm,),
        in_specs=[_row_spec(tm, D), ms, ms, _full_spec((1, D)), _full_spec((D, 2 * C_WIDTH)), _full_spec((1, C_WIDTH)),
                  _full_spec((C_GROUPS, C_CHUNK, C_CHUNK)), _full_spec((C_CHUNK, LANE)), _full_spec((C_WIDTH, D)),
                  ms, ms, ms, _full_spec((1, D))],
        out_specs=[_row_spec(tm, D), _row_spec(tm, D)],
        out_shape=[jax.ShapeDtypeStruct((N, D), jnp.float32)] * 2,
        scratch_shapes=[pltpu.VMEM((tm, C_WIDTH), jnp.float32), pltpu.VMEM((tm, C_WIDTH), jnp.float32),
                        pltpu.VMEM((tm, C_WIDTH), jnp.bfloat16)],
        compiler_params=_cparams(1), name="c_layer",
    )(x, mk[0], mk[1], g1.reshape(1, D), w_in_b, vnorm_g.reshape(1, C_WIDTH), w_s_b, bs, w_out_b,
      mk[2], mk[3], mk[4], g2.reshape(1, D))


def _split3(x):
    hi = x.astype(jnp.bfloat16)
    r = x - hi.astype(jnp.float32)
    mid = r.astype(jnp.bfloat16)
    lo = (r - mid.astype(jnp.float32)).astype(jnp.bfloat16)
    return hi, mid, lo


def _unit_tri_inverse_offdiag(a, ii, jj):
    def same_block(b):
        return jnp.right_shift(ii, b) == jnp.right_shift(jj, b)
    n = -jnp.where(same_block(1), a, 0.0)
    for b in range(1, 7):
        o = jnp.where(same_block(b + 1) & jnp.logical_not(same_block(b)), a, 0.0)
        p = o + _bdot(n, o)
        q = p + _bdot(p, n)
        n = n - q
    return n


DELTA_GROUP = 2


def _delta_kernel(q_ref, k_ref, v_ref, gate_ref, ab_ref, abt_ref, cw_ref, prow_ref, pcol_ref, og_ref, s0_ref,
                  o_ref, sfin_ref, u_s, w_s, qk_s, qd_s, ktt_s, egl_s, of_s, ob_s, *, use_s0):
    T = q_ref.shape[1]
    nck = T // DC
    h = pl.program_id(1)
    ii = lax.broadcasted_iota(jnp.int32, (DC, DC), 0)
    jj = lax.broadcasted_iota(jnp.int32, (DC, DC), 1)
    low = (jj <= ii).astype(jnp.bfloat16)
    upp = (jj >= ii).astype(jnp.bfloat16)
    tri_cols = jnp.concatenate([low, upp], axis=0)
    tri_rows = jnp.concatenate([upp, low], axis=1)
    r16 = lax.broadcasted_iota(jnp.int32, (AB_COLS, DC), 0)
    i2 = lax.broadcasted_iota(jnp.int32, (2 * DC, 2 * DC), 0)
    j2 = lax.broadcasted_iota(jnp.int32, (2 * DC, 2 * DC), 1)

    def shifted(ref, c, d):
        cur = ref[0, pl.ds(pl.multiple_of(c * DC, DC), DC), :]
        if d == 0:
            return cur
        step = 1 if d > 0 else -1
        cn = jnp.clip(c + step, 0, nck - 1)
        nb = ref[0, pl.ds(pl.multiple_of(cn * DC, DC), DC), :]
        ok = jnp.where((c + step >= 0) & (c + step < nck), 1.0, 0.0)
        rc = pltpu.roll(cur, (-d) % DC, axis=0)
        rn = pltpu.roll(nb, (-d) % DC, axis=0) * ok
        return jnp.where((ii + d >= 0) & (ii + d < DC), rc, rn)

    def prep_chunk(c):
        rows = pl.ds(pl.multiple_of(c * DC, DC), DC)

        def conv_silu(ref, idx):
            acc = jnp.zeros((DC, HEAD_DIM), jnp.float32)
            for kk in range(A_CONV):
                acc = acc + shifted(ref, c, kk - A_CONV // 2) * cw_ref[0, idx, kk:kk + 1, :]
            return acc * _sigmoid(acc)

        q = conv_silu(q_ref, 0)
        k = conv_silu(k_ref, 1)
        v = conv_silu(v_ref, 2)
        q = q * lax.rsqrt(jnp.sum(q * q, axis=-1, keepdims=True) + EPS) * (HEAD_DIM ** -0.5)
        k = k * lax.rsqrt(jnp.sum(k * k, axis=-1, keepdims=True) + EPS)
        kt = k.T
        ab = ab_ref[0, rows, :]
        g_all = -prow_ref[0:1, :] * _softplus(ab + prow_ref[1:2, :])
        beta_all = _sigmoid(ab)
        abt = abt_ref[:, rows]
        gr_all = -pcol_ref[0] * _softplus(abt + pcol_ref[1])

        def pick_col(x, col):
            return jnp.sum(jnp.where(jj == col, x, 0.0), axis=1, keepdims=True)

        def pick_row(x, row):
            return jnp.sum(jnp.where(r16 == row, x, 0.0), axis=0, keepdims=True)

        qkt = _bdot(q, kt)
        f32 = jnp.float32
        gcs = sum(jnp.dot(tri_cols, t, preferred_element_type=f32) for t in _split3(g_all))
        grs = sum(jnp.dot(t, tri_rows, preferred_element_type=f32) for t in _split3(gr_all))
        per_dir = []
        for dirn in range(2):
            col = dirn * A_HEADS + h
            gc = pick_col(gcs[dirn * DC:(dirn + 1) * DC], col)
            gr = pick_row(grs[:, dirn * DC:(dirn + 1) * DC], col)
            beta = pick_col(beta_all, 2 * A_HEADS + col)
            tri = (jj <= ii) if dirn == 0 else (jj >= ii)
            gamma = jnp.where(tri, jnp.exp(jnp.where(tri, gc - gr, 0.0)), 0.0)
            kb = k * beta
            egc = jnp.exp(gc)
            gl = gc[DC - 1:DC, :] if dirn == 0 else gc[0:1, :]
            qk_s[dirn, c] = qkt * gamma
            qd_s[dirn, c] = q * egc
            ktt_s[dirn, c] = kt * jnp.exp(gl - gr)
            egl_s[dirn, c] = jnp.broadcast_to(jnp.exp(gl), (8, DC))
            per_dir.append((kb, gamma, jnp.concatenate([v * beta, kb * egc], axis=1)))
        kk = _bdot(jnp.concatenate([per_dir[0][0], per_dir[1][0]], axis=0), kt)
        zero = jnp.zeros((DC, DC), jnp.float32)
        a_f = jnp.where(jj < ii, kk[:DC] * per_dir[0][1], 0.0)
        a_b = jnp.where(jj > ii, kk[DC:] * per_dir[1][1], 0.0)
        a = jnp.concatenate([jnp.concatenate([a_f, zero], axis=1), jnp.concatenate([zero, a_b], axis=1)], axis=0)
        n = _unit_tri_inverse_offdiag(a, i2, j2)
        rhs = jnp.concatenate([per_dir[0][2], per_dir[1][2]], axis=0)
        sol = rhs + _bdot(n, rhs)
        for dirn in range(2):
            u_s[dirn, c] = sol[dirn * DC:(dirn + 1) * DC, :HEAD_DIM]
            w_s[dirn, c] = sol[dirn * DC:(dirn + 1) * DC, HEAD_DIM:]

    def prep(cg, carry):
        for g in range(DELTA_GROUP):
            prep_chunk(cg * DELTA_GROUP + g)
        return carry

    lax.fori_loop(0, nck // DELTA_GROUP, prep, 0)

    def chunk_update(S, c, dirn):
        v_new = u_s[dirn, c] - _bdot(w_s[dirn, c], S)
        o = _bdot(qd_s[dirn, c], S) + _bdot(qk_s[dirn, c], v_new)
        S = S * egl_s[dirn, c, 0:1, :] + _bdot(ktt_s[dirn, c], v_new)
        return S, o

    def step(i, carry):
        sf, sb = carry
        sf, o_f = chunk_update(sf, i, 0)
        of_s[pl.ds(pl.multiple_of(i * DC, DC), DC), :] = o_f
        cb = nck - 1 - i
        sb, o_b = chunk_update(sb, cb, 1)
        ob_s[pl.ds(pl.multiple_of(cb * DC, DC), DC), :] = o_b
        return sf, sb

    if use_s0:
        init = (s0_ref[0, 0, 0], s0_ref[0, 1, 0])
    else:
        init = (jnp.zeros((HEAD_DIM, HEAD_DIM), jnp.float32), jnp.zeros((HEAD_DIM, HEAD_DIM), jnp.float32))
    sf, sb = lax.fori_loop(0, nck, step, init)
    sfin_ref[0, 0, 0] = sf
    sfin_ref[0, 1, 0] = sb

    def fin(c, carry):
        rows = pl.ds(pl.multiple_of(c * DC, DC), DC)
        o = of_s[rows, :] + ob_s[rows, :]
        o = _rms(o) * og_ref[...]
        g = gate_ref[0, rows, :]
        o_ref[0, rows, :] = o * (g * _sigmoid(g))
        return carry

    lax.fori_loop(0, nck, fin, 0)


def delta_mixer_pallas(qkv, gate, ab, abt, cw, prow, pcol, onorm_g, s0):
    B, T, _ = qkv.shape
    H = A_HEADS
    state_block = (1, 2, 1, HEAD_DIM, HEAD_DIM)
    use_s0 = s0 is not None
    if use_s0:
        s0_spec = pl.BlockSpec(state_block, lambda b, h: (b, 0, h, 0, 0))
    else:
        s0 = jnp.zeros(state_block, jnp.float32)
        s0_spec = _full_spec(state_block)
    nck = T // DC
    seq = lambda off: pl.BlockSpec((1, T, HEAD_DIM), lambda b, h: (b, 0, off + h))
    assert nck % DELTA_GROUP == 0
    tok = pltpu.VMEM((T, HEAD_DIM), jnp.float32)
    blk = pltpu.VMEM((2, nck, DC, HEAD_DIM), jnp.float32)
    return pl.pallas_call(
        functools.partial(_delta_kernel, use_s0=use_s0),
        grid=(B, H),
        in_specs=[seq(0), seq(H), seq(2 * H), seq(0),
                  pl.BlockSpec((1, T, LANE), lambda b, h: (b, 0, 0)), pl.BlockSpec((AB_COLS, T), lambda b, h: (0, b)),
                  pl.BlockSpec((1, 3, A_CONV, HEAD_DIM), lambda b, h: (h, 0, 0, 0)),
                  _full_spec((2, LANE)), _full_spec((2, AB_COLS, LANE)), _full_spec((1, HEAD_DIM)), s0_spec],
        out_specs=[seq(0), pl.BlockSpec(state_block, lambda b, h: (b, 0, h, 0, 0))],
        out_shape=[jax.ShapeDtypeStruct((B, T, A_W), jnp.float32),
                   jax.ShapeDtypeStruct((B, 2, H, HEAD_DIM, HEAD_DIM), jnp.float32)],
        scratch_shapes=[blk, blk, blk, blk, blk, pltpu.VMEM((2, nck, 8, DC), jnp.float32), tok, tok],
        compiler_params=_cparams(2), name="delta",
    )(qkv, qkv, qkv, gate, ab, abt, cw, prow, pcol, onorm_g, s0)


def _softmax_segments(scores, values, sink):
    m = jnp.full((scores[0].shape[0], 1), sink, jnp.float32)
    for s in scores:
        m = jnp.maximum(m, jnp.max(s, axis=-1, keepdims=True))
    den = jnp.exp(sink - m)
    o = None
    for s, v in zip(scores, values):
        p = jnp.exp(s - m)
        den = den + jnp.sum(p, axis=-1, keepdims=True)
        pv = _bdot(p, v)
        o = pv if o is None else o + pv
    return o / den


def _ctx_attn_kernel(sink_ref, q_ref, k_ref, v_ref, o_ref):
    hkv = pl.program_id(1)
    k = k_ref[0]
    v = v_ref[0]
    for g in range(B_GROUP):
        q = q_ref[0, :, g * HEAD_DIM:(g + 1) * HEAD_DIM]
        s = _bdot_nt(q, k) * (HEAD_DIM ** -0.5)
        o_ref[0, :, g * HEAD_DIM:(g + 1) * HEAD_DIM] = _softmax_segments([s], [v], sink_ref[hkv * B_GROUP + g])


def context_attention_pallas(qb, kb, vb, sink):
    B, S, _ = qb.shape
    gw = B_GROUP * HEAD_DIM
    return pl.pallas_call(
        _ctx_attn_kernel,
        grid=(B, B_KV_HEADS),
        in_specs=[pl.BlockSpec(memory_space=pltpu.SMEM),
                  pl.BlockSpec((1, S, gw), lambda b, h: (b, 0, h)),
                  pl.BlockSpec((1, S, HEAD_DIM), lambda b, h: (b, 0, h)),
                  pl.BlockSpec((1, S, HEAD_DIM), lambda b, h: (b, 0, h))],
        out_specs=pl.BlockSpec((1, S, gw), lambda b, h: (b, 0, h)),
        out_shape=jax.ShapeDtypeStruct(qb.shape, jnp.float32),
        compiler_params=_cparams(2), name="ctx_attn",
    )(sink, qb, kb, vb)


def rope_tables(n_tokens):
    t = np.arange((n_tokens // GRID_W) * GRID_W)
    row = jnp.asarray((t // GRID_W).astype(np.float32))
    col = jnp.asarray((t % GRID_W).astype(np.float32))
    nf = HEAD_DIM // 4
    inv = ROPE_BASE ** (-jnp.arange(nf, dtype=jnp.float32) / nf)
    ar = row[:, None] * inv
    ac = col[:, None] * inv
    rc = jnp.concatenate([jnp.cos(ar), jnp.cos(ar), jnp.cos(ac), jnp.cos(ac)], axis=-1)
    rs = jnp.concatenate([-jnp.sin(ar), jnp.sin(ar), -jnp.sin(ac), jnp.sin(ac)], axis=-1)
    return rc, rs


def _rope(x, rc, rs):
    lane = lax.broadcasted_iota(jnp.int32, x.shape, 1)
    nf = HEAD_DIM // 4
    swapped = jnp.where((lane & (2 * nf - 1)) < nf, pltpu.roll(x, HEAD_DIM - nf, axis=1), pltpu.roll(x, nf, axis=1))
    return x * rc + swapped * rs


def _lat_attn_kernel(sink_ref, q_ref, kp_ref, kc_ref, kn_ref, vp_ref, vc_ref, vn_ref, kx_ref, vx_ref,
                     rcp_ref, rcc_ref, rcn_ref, rsp_ref, rsc_ref, rsn_ref, o_ref):
    hkv = pl.program_id(1)
    i = pl.program_id(2)
    nb = pl.num_programs(2)
    r = lax.broadcasted_iota(jnp.int32, (ATTN_BLOCK, ATTN_BLOCK), 0)
    c = lax.broadcasted_iota(jnp.int32, (ATTN_BLOCK, ATTN_BLOCK), 1)
    kp = _rope(kp_ref[0], rcp_ref[...], rsp_ref[...])
    kc = _rope(kc_ref[0], rcc_ref[...], rsc_ref[...])
    kn = _rope(kn_ref[0], rcn_ref[...], rsn_ref[...])
    ok_p = (c >= r) & (i > 0)
    ok_n = (c <= r) & (i < nb - 1)
    scale = HEAD_DIM ** -0.5
    for g in range(B_GROUP):
        q = _rope(q_ref[0, :, g * HEAD_DIM:(g + 1) * HEAD_DIM], rcc_ref[...], rsc_ref[...])
        sx = _bdot_nt(q, kx_ref[0]) * scale
        sp = jnp.where(ok_p, _bdot_nt(q, kp) * scale, NEG_INF)
        sc = _bdot_nt(q, kc) * scale
        sn = jnp.where(ok_n, _bdot_nt(q, kn) * scale, NEG_INF)
        o_ref[0, :, g * HEAD_DIM:(g + 1) * HEAD_DIM] = _softmax_segments(
            [sx, sp, sc, sn], [vx_ref[0], vp_ref[0], vc_ref[0], vn_ref[0]], sink_ref[hkv * B_GROUP + g])


def latent_attention_pallas(qb, kb, vb, k_ctx, v_ctx, sink):
    assert WINDOW == ATTN_BLOCK
    B, T, _ = qb.shape
    P = k_ctx.shape[1]
    nb = T // ATTN_BLOCK
    gw = B_GROUP * HEAD_DIM
    rc, rs = rope_tables(T)
    prev = lambda b, h, i: (b, jnp.maximum(i - 1, 0), h)
    cur = lambda b, h, i: (b, i, h)
    nxt = lambda b, h, i: (b, jnp.minimum(i + 1, nb - 1), h)
    blk = lambda f: pl.BlockSpec((1, ATTN_BLOCK, HEAD_DIM), f)
    tab = lambda f: pl.BlockSpec((ATTN_BLOCK, HEAD_DIM), lambda b, h, i: (f(b, h, i)[1], 0))
    ctx = pl.BlockSpec((1, P, HEAD_DIM), lambda b, h, i: (b, 0, h))
    return pl.pallas_call(
        _lat_attn_kernel,
        grid=(B, B_KV_HEADS, nb),
        in_specs=[pl.BlockSpec(memory_space=pltpu.SMEM), pl.BlockSpec((1, ATTN_BLOCK, gw), cur),
                  blk(prev), blk(cur), blk(nxt), blk(prev), blk(cur), blk(nxt), ctx, ctx,
                  tab(prev), tab(cur), tab(nxt), tab(prev), tab(cur), tab(nxt)],
        out_specs=pl.BlockSpec((1, ATTN_BLOCK, gw), cur),
        out_shape=jax.ShapeDtypeStruct(qb.shape, jnp.float32),
        compiler_params=_cparams(3), name="lat_attn",
    )(sink, qb, kb, kb, kb, vb, vb, vb, k_ctx, v_ctx, rc, rc, rc, rs, rs, rs)


def _top16(s, iota, iota16, want_rank):
    cur = s
    rank = jnp.full(s.shape, float(PEER_TOPK), jnp.float32) if want_rank else None
    sv = jnp.zeros((PEER_TOPK, s.shape[1]), jnp.float32)
    si = jnp.zeros((PEER_TOPK, s.shape[1]), jnp.float32)
    for k in range(PEER_TOPK):
        m = jnp.max(cur, axis=0, keepdims=True)
        idx = jnp.min(jnp.where(cur == m, iota, float(N_KEYS)), axis=0, keepdims=True)
        sel = iota == idx
        if want_rank:
            rank = jnp.where(sel, float(k), rank)
        else:
            si = jnp.where(iota16 == float(k), idx, si)
        cur = jnp.where(sel, -jnp.inf, cur)
        sv = jnp.where(iota16 == float(k), m, sv)
    return sv, (rank if want_rank else si)


def _top16_distinct(s, iota16):
    cur = s
    rank = jnp.full(s.shape, float(PEER_TOPK), jnp.float32)
    sv = jnp.zeros((PEER_TOPK, s.shape[1]), jnp.float32)
    for k in range(PEER_TOPK):
        m = jnp.max(cur, axis=0, keepdims=True)
        sel = cur == m
        rank = jnp.where(sel, float(k), rank)
        cur = jnp.where(sel, -jnp.inf, cur)
        sv = jnp.where(iota16 == float(k), m, sv)
    return sv, rank


def _staircase(sv0, sv1, iota16):
    n = jnp.zeros_like(sv0)
    f = sv0 + sv1[0:1, :]
    top = f[0:1, :]
    z = jnp.zeros_like(top)
    for _ in range(PEER_TOPK):
        fm = jnp.max(f, axis=0, keepdims=True)
        cand = iota16 * float(PEER_TOPK) + n
        ci = jnp.min(jnp.where(f == fm, cand, 1e9), axis=0, keepdims=True)
        win = cand == ci
        z = z + jnp.exp(fm - top)
        n = jnp.where(win, n + 1.0, n)
        nw = jnp.max(jnp.where(win, n, 0.0), axis=0, keepdims=True)
        nxt = jnp.max(jnp.where(iota16 == nw, sv1, -jnp.inf), axis=0, keepdims=True)
        f = jnp.where(win, sv0 + nxt, f)
    return n, z


def _peer_route_kernel(h_ref, wq_ref, keys_ref, ht_ref, r1_ref, p1_ref, c0_ref, q0_ref, q_scr, s_scr):
    tm = h_ref.shape[0]
    hf = h_ref[...]
    ht_ref[...] = hf.T.astype(jnp.bfloat16)
    q_scr[...] = jnp.dot(hf.astype(jnp.bfloat16), wq_ref[...], preferred_element_type=jnp.float32)

    def head_body(h, carry):
        for p in range(2):
            off = pl.multiple_of(h * D_KEY + p * (D_KEY // 2), LANE)
            s_scr[p] = _bdot_nt(keys_ref[p], q_scr[:, pl.ds(off, D_KEY // 2)])

        iota = lax.broadcasted_iota(jnp.int32, (N_KEYS, LANE), 0).astype(jnp.float32)
        iota16 = lax.broadcasted_iota(jnp.int32, (PEER_TOPK, LANE), 0).astype(jnp.float32)

        def emit(lanes, s0, s1, sv0, sv1, rank1, c0_terms):
            n, z = _staircase(sv0, sv1, iota16)
            c0 = jnp.zeros_like(s0)
            for k in range(PEER_TOPK):
                c0 = jnp.where(c0_terms(k), n[k:k + 1, :], c0)
            p1 = jnp.exp(s1 - sv1[0:1, :])
            r1_ref[h, :, lanes] = pltpu.bitcast(rank1.astype(jnp.bfloat16), jnp.uint32)
            p1_ref[h, :, lanes] = pltpu.bitcast(p1.astype(jnp.bfloat16), jnp.uint32)
            c0_ref[h, :, lanes] = c0
            q0_ref[h, :, lanes] = jnp.exp(s0 - sv0[0:1, :]) / z

        picked = None
        for c in range(tm // LANE):
            lanes = slice(c * LANE, (c + 1) * LANE)
            s0 = s_scr[0, :, lanes]
            s1 = s_scr[1, :, lanes]
            sv0, rank0 = _top16_distinct(s0, iota16)
            sv1, rank1 = _top16_distinct(s1, iota16)
            emit(lanes, s0, s1, sv0, sv1, rank1, lambda k: rank0 == float(k))
            cnt = (jnp.sum(jnp.where(rank0 < float(PEER_TOPK), 1.0, 0.0), axis=0, keepdims=True)
                   + jnp.sum(jnp.where(rank1 < float(PEER_TOPK), 1.0, 0.0), axis=0, keepdims=True))
            picked = cnt if picked is None else jnp.maximum(picked, cnt)

        @pl.when(jnp.max(picked) > float(2 * PEER_TOPK))
        def _():
            def exact_chunk(c, carry2):
                lanes = pl.ds(pl.multiple_of(c * LANE, LANE), LANE)
                s0 = s_scr[0, :, lanes]
                s1 = s_scr[1, :, lanes]
                sv0, si0 = _top16(s0, iota, iota16, False)
                sv1, rank1 = _top16(s1, iota, iota16, True)
                emit(lanes, s0, s1, sv0, sv1, rank1, lambda k: iota == si0[k:k + 1, :])
                return carry2

            lax.fori_loop(0, tm // LANE, exact_chunk, 0)

        return carry

    lax.fori_loop(0, PEER_HEADS, head_body, 0)


def _peer_expert_kernel(ht_ref, r1_ref, p1_ref, c0_ref, q0_ref, ed_ref, eut_ref, x_ref, gt_ref, fg_ref, out_ref,
                        acc_ref, act_ref, wg_ref, *, final_norm):
    e = pl.program_id(1)
    tm = ht_ref.shape[1]
    te = ed_ref.shape[0]
    bf16 = jnp.bfloat16

    @pl.when(e == 0)
    def _():
        acc_ref[...] = jnp.zeros_like(acc_ref)

    for sb in range(te // EXPERT_SUB):
        rs = slice(sb * EXPERT_SUB, (sb + 1) * EXPERT_SUB)
        act_ref[rs, :] = jnp.dot(ed_ref[rs, :], ht_ref[...], preferred_element_type=jnp.float32)
    half = N_KEYS // 2
    for ii in range(te // N_KEYS):
        for c in range(tm // LANE):
            lanes = slice(c * LANE, (c + 1) * LANE)
            g = jnp.zeros((N_KEYS, LANE), bf16)
            for h in range(PEER_HEADS):
                c0b = jnp.broadcast_to(c0_ref[h, ii:ii + 1, lanes], (N_KEYS, LANE)).astype(bf16)
                q0b = jnp.broadcast_to(q0_ref[h, ii:ii + 1, lanes], (N_KEYS, LANE)).astype(bf16)
                r1b = pltpu.bitcast(r1_ref[h, :, lanes], bf16)
                p1b = pltpu.bitcast(p1_ref[h, :, lanes], bf16)
                g = g + jnp.where(r1b < c0b, p1b, jnp.zeros((), bf16)) * q0b
            ab = _gelu_tanh(act_ref[ii * N_KEYS:(ii + 1) * N_KEYS, lanes].astype(bf16))
            wg_ref[ii * half:(ii + 1) * half, lanes] = pltpu.bitcast(g * ab, jnp.uint32)
    for ub in range(te // EXPERT_UP):
        acc_ref[...] += jnp.dot(eut_ref[:, ub * EXPERT_UP:(ub + 1) * EXPERT_UP],
                                pltpu.bitcast(wg_ref[ub * EXPERT_UP // 2:(ub + 1) * EXPERT_UP // 2, :], bf16),
                                preferred_element_type=jnp.float32)

    @pl.when(e == pl.num_programs(1) - 1)
    def _():
        y = x_ref[...] + gt_ref[0] * acc_ref[...].T
        if final_norm:
            y = _rms(y) * fg_ref[...]
        out_ref[...] = y


def peer_pallas(h, x, gate, wq_b, keys, ed_b, eut_b, final_g, final_norm, layer):
    T, D = h.shape
    nq = PEER_HEADS * D_KEY
    route_shape = jax.ShapeDtypeStruct((PEER_HEADS, N_KEYS, T), jnp.float32)
    route_shape_b = jax.ShapeDtypeStruct((PEER_HEADS, N_KEYS // 2, T), jnp.uint32)
    tmr = ROUTE_TM
    route_spec = pl.BlockSpec((PEER_HEADS, N_KEYS, tmr), lambda i: (0, 0, i))
    route_spec_b = pl.BlockSpec((PEER_HEADS, N_KEYS // 2, tmr), lambda i: (0, 0, i))
    ht, r1, p1, c0, q0 = pl.pallas_call(
        _peer_route_kernel,
        grid=(T // tmr,),
        in_specs=[_row_spec(tmr, D), pl.BlockSpec((None, D, nq), lambda i: (layer, 0, 0)),
                  pl.BlockSpec((None, 2, N_KEYS, D_KEY // 2), lambda i: (layer, 0, 0, 0))],
        out_specs=[pl.BlockSpec((D, tmr), lambda i: (0, i)), route_spec_b, route_spec_b, route_spec, route_spec],
        out_shape=[jax.ShapeDtypeStruct((D, T), jnp.bfloat16), route_shape_b, route_shape_b, route_shape, route_shape],
        scratch_shapes=[pltpu.VMEM((tmr, nq), jnp.float32), pltpu.VMEM((2, N_KEYS, tmr), jnp.float32)],
        compiler_params=_cparams(1), name="peer_route",
    )(h, wq_b, keys)

    tm, te = EXPERT_TM, EXPERT_TE
    packed_spec = pl.BlockSpec((PEER_HEADS, N_KEYS // 2, tm), lambda i, e: (0, 0, i))
    row_spec = pl.BlockSpec((PEER_HEADS, te // N_KEYS, tm), lambda i, e: (0, e, i))
    return pl.pallas_call(
        functools.partial(_peer_expert_kernel, final_norm=final_norm),
        grid=(T // tm, N_EXPERTS // te),
        in_specs=[pl.BlockSpec((D, tm), lambda i, e: (0, i)), packed_spec, packed_spec, row_spec, row_spec,
                  pl.BlockSpec((None, te, D), lambda i, e: (layer, e, 0)),
                  pl.BlockSpec((None, D, te), lambda i, e: (layer, 0, e)),
                  pl.BlockSpec((tm, D), lambda i, e: (i, 0)), _mod_spec(tm, T // gate.shape[0]), _full_spec((1, D))],
        out_specs=pl.BlockSpec((tm, D), lambda i, e: (i, 0)),
        out_shape=jax.ShapeDtypeStruct((T, D), jnp.float32),
        scratch_shapes=[pltpu.VMEM((D, tm), jnp.float32), pltpu.VMEM((te, tm), jnp.float32), pltpu.VMEM((te // 2, tm), jnp.uint32)],
        compiler_params=_cparams(2), name="peer_expert",
    )(ht, r1, p1, c0, q0, ed_b, eut_b, x, gate, final_g.reshape(1, D))


def kernel(x_prompt, x_sample, cache_attn_k, cache_attn_v, state_delta, c, c_ctx, w_mod, b_mod, norm_g, w_in_ab, conv_ab, a_log_ab, dt_bias_ab, onorm_ab, sink_ab, w_out_ab, w_in_c, vnorm_c, w_s_c, b_s_c, w_out_c, peer_wq, peer_keys, peer_down, peer_up, final_norm_g):
    f32, bf16 = jnp.float32, jnp.bfloat16
    BP, SP, D = x_prompt.shape
    BS, TS, _ = x_sample.shape
    past = cache_attn_k.shape[2]
    H = A_HEADS
    cvecs = jnp.zeros((8, D), f32).at[0].set(c_ctx).at[1:1 + BS].set(c)
    streams = [dict(x=x_prompt.reshape(BP * SP, D), B=BP, T=SP, rows=slice(0, 1)),
               dict(x=x_sample.reshape(BS * TS, D), B=BS, T=TS, rows=slice(1, 1 + BS))]
    new_k = new_v = new_s = None
    wq_b = peer_wq.astype(bf16)
    ed_b = peer_down.astype(bf16)
    eut_b = jnp.swapaxes(peer_up, 1, 2).astype(bf16)
    for layer in range(DEPTH):
        m = modulation_pallas(cvecs, w_mod[layer], b_mod[layer])
        li = layer // 2
        last = layer == DEPTH - 1
        if layer % 2 == 0:
            w_in = w_in_ab[li]
            c0 = QKV_W + A_W
            w_main = jnp.concatenate([w_in[:, :c0], w_in[:, c0 + AB_COLS:]], axis=1).astype(bf16)
            w_log = w_in[:, c0:c0 + AB_COLS].astype(bf16)
            w_log_pad = jnp.pad(w_log, ((0, 0), (0, LANE - AB_COLS)))
            w_out_b = w_out_ab[li].astype(bf16)
            cw = conv_ab[li].reshape(A_CONV, 3, H, HEAD_DIM).transpose(2, 1, 0, 3)
            a_dec = jnp.exp(a_log_ab[li].astype(f32)).reshape(2 * H)
            dt = dt_bias_ab[li].astype(f32).reshape(2 * H)
            prow = jnp.zeros((2, LANE), f32).at[0, :2 * H].set(a_dec).at[1, :2 * H].set(dt)
            pcol = jnp.zeros((2, AB_COLS, LANE), f32).at[0, :2 * H].set(a_dec[:, None]).at[1, :2 * H].set(dt[:, None])
            og = onorm_ab[li].reshape(1, HEAD_DIM).astype(f32)
            sink = sink_ab[li].astype(f32)
        else:
            w_in_b = w_in_c[li].astype(bf16)
            w_s_b = w_s_c[li].astype(bf16)
            bs = jnp.pad(b_s_c[li].T.astype(f32), ((0, 0), (0, LANE - C_GROUPS)))
            w_out_b = w_out_c[li].astype(bf16)
        for si, st in enumerate(streams):
            B, T = st["B"], st["T"]
            mk = [m[st["rows"], k * D:(k + 1) * D].reshape(-1, 1, D) for k in range(6)]
            x = st["x"]
            if layer % 2 == 0:
                qkv, gate, qb, kb, vb, ab, abt = ab_in_pallas(x, mk[0], mk[1], norm_g[layer, 0], w_main, w_log_pad, w_log.T)
                s0 = None if si == 0 else state_delta[:, li]
                o_a, s_fin = delta_mixer_pallas(qkv.reshape(B, T, QKV_W), gate.reshape(B, T, A_W), ab.reshape(B, T, LANE),
                                                abt, cw, prow, pcol, og, s0)
                if si == 0:
                    o_b = context_attention_pallas(qb.reshape(B, T, QB_W), kb.reshape(B, T, KV_W), vb.reshape(B, T, KV_W), sink)
                    new_k = kb.reshape(B, 1, T, B_KV_HEADS, HEAD_DIM)
                    new_v = vb.reshape(B, 1, T, B_KV_HEADS, HEAD_DIM)
                    new_s = s_fin[:, None]
                else:
                    o_b = latent_attention_pallas(qb.reshape(B, T, QB_W), kb.reshape(B, T, KV_W), vb.reshape(B, T, KV_W),
                                                  cache_attn_k[:, li].reshape(B, past, KV_W),
                                                  cache_attn_v[:, li].reshape(B, past, KV_W), sink)
                x, hn = ab_out_pallas(o_a.reshape(B * T, A_W), o_b.reshape(B * T, QB_W), w_out_b, x, mk[2], mk[3], mk[4],
                                      norm_g[layer, 1])
            else:
                x, hn = c_layer_pallas(x, mk, norm_g[layer, 0], norm_g[layer, 1], w_in_b, vnorm_c[li], w_s_b, bs, w_out_b)
            st["x"] = peer_pallas(hn, x, mk[5], wq_b, peer_keys, ed_b, eut_b, final_norm_g, last, layer)
    y_prompt = streams[0]["x"].reshape(BP, SP, D)
    y_sample = streams[1]["x"].reshape(BS, TS, D)
    return (y_prompt, y_sample, new_k, new_v, new_s)
```

```python
import functools
import math

import numpy as np
import jax
import jax.numpy as jnp
from jax import lax
from jax.experimental import pallas as pl
from jax.experimental.pallas import tpu as pltpu

D_MODEL = 1024
DEPTH = 2
GRID_W = 64
HEAD_DIM = 128
EPS = 1e-6
NEG_INF = -1e30
A_HEADS = D_MODEL // (2 * HEAD_DIM)
A_W = A_HEADS * HEAD_DIM
A_CONV = 5
B_HEADS = D_MODEL // (2 * HEAD_DIM)
B_KV_HEADS = B_HEADS // 2
B_GROUP = B_HEADS // B_KV_HEADS
WINDOW = 128
ATTN_BLOCK = 128
ROPE_BASE = 10000.0
C_WIDTH = D_MODEL
C_GROUPS = 8
C_GROUP_W = C_WIDTH // C_GROUPS
C_CHUNK = 128
PEER_HEADS = 8
N_KEYS = 128
N_EXPERTS = N_KEYS * N_KEYS
PEER_TOPK = 16
D_KEY = 256

LANE = 128
VMEM_LIMIT = 48 * 1024 * 1024
PROJ_TM = 512
ROUTE_TM = 512
EXPERT_TM = 512
EXPERT_TE = 2048
EXPERT_SUB = 256
EXPERT_UP = 1024
MOD_TN = 1536
DC = 128

QKV_W = 3 * A_W
QB_W = B_HEADS * HEAD_DIM
KV_W = B_KV_HEADS * HEAD_DIM
AB_COLS = 4 * A_HEADS
AB_MAIN = QKV_W + A_W + QB_W + 2 * KV_W


def _bdot(a, b):
    return jnp.dot(a.astype(jnp.bfloat16), b.astype(jnp.bfloat16), preferred_element_type=jnp.float32)


def _bdot_nt(a, b):
    return lax.dot_general(a.astype(jnp.bfloat16), b.astype(jnp.bfloat16), (((1,), (1,)), ((), ())),
                           preferred_element_type=jnp.float32)


def _sigmoid(x):
    return 1.0 / (1.0 + jnp.exp(-x))


def _softplus(x):
    return jnp.maximum(x, 0.0) + jnp.log(1.0 + jnp.exp(-jnp.abs(x)))


def _gelu_tanh(x):
    k0 = -2.0 * math.sqrt(2.0 / math.pi) * math.log2(math.e)
    return x / (1.0 + jnp.exp2(x * (k0 + (k0 * 0.044715) * (x * x))))


def _rms(x):
    return x * lax.rsqrt(jnp.mean(x * x, axis=-1, keepdims=True) + EPS)


def _modnorm(x, g, shift, scale):
    return _rms(x) * g * (1.0 + scale) + shift


def _cparams(n):
    return pltpu.CompilerParams(dimension_semantics=("arbitrary",) * n, vmem_limit_bytes=VMEM_LIMIT)


def _mod_spec(tm, tokens_per_mod):
    return pl.BlockSpec((1, 1, D_MODEL), lambda i, *_: ((i * tm) // tokens_per_mod, 0, 0))


def _row_spec(tm, width):
    return pl.BlockSpec((tm, width), lambda i: (i, 0))


def _full_spec(shape):
    return pl.BlockSpec(shape, lambda *_: (0,) * len(shape))


def _mod_kernel(c_ref, w_ref, b_ref, o_ref):
    c = c_ref[...]
    o_ref[...] = _bdot(c * _sigmoid(c), w_ref[...]) + b_ref[...]


def modulation_pallas(cvecs, w_mod, b_mod):
    R, D = cvecs.shape
    N = w_mod.shape[1]
    return pl.pallas_call(
        _mod_kernel, grid=(N // MOD_TN,),
        in_specs=[pl.BlockSpec((R, D), lambda j: (0, 0)), pl.BlockSpec((D, MOD_TN), lambda j: (0, j)),
                  pl.BlockSpec((1, MOD_TN), lambda j: (0, j))],
        out_specs=pl.BlockSpec((R, MOD_TN), lambda j: (0, j)),
        out_shape=jax.ShapeDtypeStruct((R, N), jnp.float32),
        compiler_params=_cparams(1), name="modulation",
    )(cvecs, w_mod, b_mod.reshape(1, N))


def _ab_in_kernel(x_ref, sh_ref, sc_ref, g_ref, w_ref, wl_ref, wlt_ref,
                  qkv_ref, gate_ref, qb_ref, kb_ref, vb_ref, ab_ref, abt_ref):
    hb = _modnorm(x_ref[...], g_ref[...], sh_ref[0], sc_ref[0]).astype(jnp.bfloat16)
    res = jnp.dot(hb, w_ref[...], preferred_element_type=jnp.float32)
    o = 0
    for ref, wd in ((qkv_ref, QKV_W), (gate_ref, A_W), (qb_ref, QB_W), (kb_ref, KV_W), (vb_ref, KV_W)):
        ref[...] = res[:, o:o + wd]
        o += wd
    ab_ref[...] = jnp.dot(hb, wl_ref[...], preferred_element_type=jnp.float32)
    abt_ref[...] = lax.dot_general(wlt_ref[...], hb, (((1,), (1,)), ((), ())), preferred_element_type=jnp.float32)


def ab_in_pallas(x, shift, scale, g, w_main, w_log_pad, w_log_t):
    N, D = x.shape
    tm = PROJ_TM
    ms = _mod_spec(tm, N // shift.shape[0])
    outs = [(N, QKV_W), (N, A_W), (N, QB_W), (N, KV_W), (N, KV_W), (N, LANE)]
    return pl.pallas_call(
        _ab_in_kernel, grid=(N // tm,),
        in_specs=[_row_spec(tm, D), ms, ms, _full_spec((1, D)), _full_spec((D, AB_MAIN)), _full_spec((D, LANE)),
                  _full_spec((AB_COLS, D))],
        out_specs=[_row_spec(tm, s[1]) for s in outs] + [pl.BlockSpec((AB_COLS, tm), lambda i: (0, i))],
        out_shape=[jax.ShapeDtypeStruct(s, jnp.float32) for s in outs] + [jax.ShapeDtypeStruct((AB_COLS, N), jnp.float32)],
        compiler_params=_cparams(1), name="ab_in",
    )(x, shift, scale, g.reshape(1, D), w_main, w_log_pad, w_log_t)


def _ab_out_kernel(oa_ref, ob_ref, w_ref, x_ref, gt_ref, sh_ref, sc_ref, g_ref, xn_ref, hn_ref):
    res = _bdot(oa_ref[...], w_ref[0:A_W, :]) + _bdot(ob_ref[...], w_ref[A_W:, :])
    xn = x_ref[...] + gt_ref[0] * res
    xn_ref[...] = xn
    hn_ref[...] = _modnorm(xn, g_ref[...], sh_ref[0], sc_ref[0])


def ab_out_pallas(oa, ob, w_out_b, x, gate, shift, scale, g):
    N, D = x.shape
    tm = PROJ_TM
    ms = _mod_spec(tm, N // gate.shape[0])
    return pl.pallas_call(
        _ab_out_kernel, grid=(N // tm,),
        in_specs=[_row_spec(tm, A_W), _row_spec(tm, QB_W), _full_spec((A_W + QB_W, D)), _row_spec(tm, D), ms, ms, ms,
                  _full_spec((1, D))],
        out_specs=[_row_spec(tm, D), _row_spec(tm, D)],
        out_shape=[jax.ShapeDtypeStruct((N, D), jnp.float32)] * 2,
        compiler_params=_cparams(1), name="ab_out",
    )(oa, ob, w_out_b, x, gate, shift, scale, g.reshape(1, D))


def _c_layer_kernel(x_ref, sh1_ref, sc1_ref, g1_ref, wi_ref, vg_ref, ws_ref, bs_ref, wo_ref, gt_ref, sh2_ref, sc2_ref,
                    g2_ref, xn_ref, hn_ref, u_ref, vn_ref, z_ref):
    tm = x_ref.shape[0]
    x = x_ref[...]
    hb = _modnorm(x, g1_ref[...], sh1_ref[0], sc1_ref[0]).astype(jnp.bfloat16)
    res = _gelu_tanh(jnp.dot(hb, wi_ref[...], preferred_element_type=jnp.float32))
    u_ref[...] = res[:, :C_WIDTH]
    vn_ref[...] = _rms(res[:, C_WIDTH:]) * vg_ref[...]
    for cc in range(tm // C_CHUNK):
        rows = slice(cc * C_CHUNK, (cc + 1) * C_CHUNK)
        for grp in range(C_GROUPS):
            cols = slice(grp * C_GROUP_W, (grp + 1) * C_GROUP_W)
            mixed = _bdot(ws_ref[grp], vn_ref[rows, cols]) + bs_ref[:, grp:grp + 1]
            z_ref[rows, cols] = (u_ref[rows, cols] * mixed).astype(jnp.bfloat16)
    out = jnp.dot(z_ref[...], wo_ref[...], preferred_element_type=jnp.float32)
    xn = x + gt_ref[0] * out
    xn_ref[...] = xn
    hn_ref[...] = _modnorm(xn, g2_ref[...], sh2_ref[0], sc2_ref[0])


def c_layer_pallas(x, mk, g1, g2, w_in_b, vnorm_g, w_s_b, bs, w_out_b):
    N, D = x.shape
    tm = PROJ_TM
    ms = _mod_spec(tm, N // mk[0].shape[0])
    return pl.pallas_call(
        _c_layer_kernel, grid=(N // tm,),
        in_specs=[_row_spec(tm, D), ms, ms, _full_spec((1, D)), _full_spec((D, 2 * C_WIDTH)), _full_spec((1, C_WIDTH)),
                  _full_spec((C_GROUPS, C_CHUNK, C_CHUNK)), _full_spec((C_CHUNK, LANE)), _full_spec((C_WIDTH, D)),
                  ms, ms, ms, _full_spec((1, D))],
        out_specs=[_row_spec(tm, D), _row_spec(tm, D)],
        out_shape=[jax.ShapeDtypeStruct((N, D), jnp.float32)] * 2,
        scratch_shapes=[pltpu.VMEM((tm, C_WIDTH), jnp.float32), pltpu.VMEM((tm, C_WIDTH), jnp.float32),
                        pltpu.VMEM((tm, C_WIDTH), jnp.bfloat16)],
        compiler_params=_cparams(1), name="c_layer",
    )(x, mk[0], mk[1], g1.reshape(1, D), w_in_b, vnorm_g.reshape(1, C_WIDTH), w_s_b, bs, w_out_b,
      mk[2], mk[3], mk[4], g2.reshape(1, D))


def _split3(x):
    hi = x.astype(jnp.bfloat16)
    r = x - hi.astype(jnp.float32)
    mid = r.astype(jnp.bfloat16)
    lo = (r - mid.astype(jnp.float32)).astype(jnp.bfloat16)
    return hi, mid, lo


def _unit_tri_inverse_offdiag(a, ii, jj):
    def same_block(b):
        return jnp.right_shift(ii, b) == jnp.right_shift(jj, b)
    n = -jnp.where(same_block(1), a, 0.0)
    for b in range(1, 7):
        o = jnp.where(same_block(b + 1) & jnp.logical_not(same_block(b)), a, 0.0)
        p = o + _bdot(n, o)
        q = p + _bdot(p, n)
        n = n - q
    return n


DELTA_GROUP = 2


def _delta_kernel(q_ref, k_ref, v_ref, gate_ref, ab_ref, abt_ref, cw_ref, prow_ref, pcol_ref, og_ref, s0_ref,
                  o_ref, sfin_ref, u_s, w_s, qk_s, qd_s, ktt_s, egl_s, of_s, ob_s, *, use_s0):
    T = q_ref.shape[1]
    nck = T // DC
    h = pl.program_id(1)
    ii = lax.broadcasted_iota(jnp.int32, (DC, DC), 0)
    jj = lax.broadcasted_iota(jnp.int32, (DC, DC), 1)
    low = (jj <= ii).astype(jnp.bfloat16)
    upp = (jj >= ii).astype(jnp.bfloat16)
    tri_cols = jnp.concatenate([low, upp], axis=0)
    tri_rows = jnp.concatenate([upp, low], axis=1)
    r16 = lax.broadcasted_iota(jnp.int32, (AB_COLS, DC), 0)
    i2 = lax.broadcasted_iota(jnp.int32, (2 * DC, 2 * DC), 0)
    j2 = lax.broadcasted_iota(jnp.int32, (2 * DC, 2 * DC), 1)

    def shifted(ref, c, d):
        cur = ref[0, pl.ds(pl.multiple_of(c * DC, DC), DC), :]
        if d == 0:
            return cur
        step = 1 if d > 0 else -1
        cn = jnp.clip(c + step, 0, nck - 1)
        nb = ref[0, pl.ds(pl.multiple_of(cn * DC, DC), DC), :]
        ok = jnp.where((c + step >= 0) & (c + step < nck), 1.0, 0.0)
        rc = pltpu.roll(cur, (-d) % DC, axis=0)
        rn = pltpu.roll(nb, (-d) % DC, axis=0) * ok
        return jnp.where((ii + d >= 0) & (ii + d < DC), rc, rn)

    def prep_chunk(c):
        rows = pl.ds(pl.multiple_of(c * DC, DC), DC)

        def conv_silu(ref, idx):
            acc = jnp.zeros((DC, HEAD_DIM), jnp.float32)
            for kk in range(A_CONV):
                acc = acc + shifted(ref, c, kk - A_CONV // 2) * cw_ref[0, idx, kk:kk + 1, :]
            return acc * _sigmoid(acc)

        q = conv_silu(q_ref, 0)
        k = conv_silu(k_ref, 1)
        v = conv_silu(v_ref, 2)
        q = q * lax.rsqrt(jnp.sum(q * q, axis=-1, keepdims=True) + EPS) * (HEAD_DIM ** -0.5)
        k = k * lax.rsqrt(jnp.sum(k * k, axis=-1, keepdims=True) + EPS)
        kt = k.T
        ab = ab_ref[0, rows, :]
        g_all = -prow_ref[0:1, :] * _softplus(ab + prow_ref[1:2, :])
        beta_all = _sigmoid(ab)
        abt = abt_ref[:, rows]
        gr_all = -pcol_ref[0] * _softplus(abt + pcol_ref[1])

        def pick_col(x, col):
            return jnp.sum(jnp.where(jj == col, x, 0.0), axis=1, keepdims=True)

        def pick_row(x, row):
            return jnp.sum(jnp.where(r16 == row, x, 0.0), axis=0, keepdims=True)

        qkt = _bdot(q, kt)
        f32 = jnp.float32
        gcs = sum(jnp.dot(tri_cols, t, preferred_element_type=f32) for t in _split3(g_all))
        grs = sum(jnp.dot(t, tri_rows, preferred_element_type=f32) for t in _split3(gr_all))
        per_dir = []
        for dirn in range(2):
            col = dirn * A_HEADS + h
            gc = pick_col(gcs[dirn * DC:(dirn + 1) * DC], col)
            gr = pick_row(grs[:, dirn * DC:(dirn + 1) * DC], col)
            beta = pick_col(beta_all, 2 * A_HEADS + col)
            tri = (jj <= ii) if dirn == 0 else (jj >= ii)
            gamma = jnp.where(tri, jnp.exp(jnp.where(tri, gc - gr, 0.0)), 0.0)
            kb = k * beta
            egc = jnp.exp(gc)
            gl = gc[DC - 1:DC, :] if dirn == 0 else gc[0:1, :]
            qk_s[dirn, c] = qkt * gamma
            qd_s[dirn, c] = q * egc
            ktt_s[dirn, c] = kt * jnp.exp(gl - gr)
            egl_s[dirn, c] = jnp.broadcast_to(jnp.exp(gl), (8, DC))
            per_dir.append((kb, gamma, jnp.concatenate([v * beta, kb * egc], axis=1)))
        kk = _bdot(jnp.concatenate([per_dir[0][0], per_dir[1][0]], axis=0), kt)
        zero = jnp.zeros((DC, DC), jnp.float32)
        a_f = jnp.where(jj < ii, kk[:DC] * per_dir[0][1], 0.0)
        a_b = jnp.where(jj > ii, kk[DC:] * per_dir[1][1], 0.0)
        a = jnp.concatenate([jnp.concatenate([a_f, zero], axis=1), jnp.concatenate([zero, a_b], axis=1)], axis=0)
        n = _unit_tri_inverse_offdiag(a, i2, j2)
        rhs = jnp.concatenate([per_dir[0][2], per_dir[1][2]], axis=0)
        sol = rhs + _bdot(n, rhs)
        for dirn in range(2):
            u_s[dirn, c] = sol[dirn * DC:(dirn + 1) * DC, :HEAD_DIM]
            w_s[dirn, c] = sol[dirn * DC:(dirn + 1) * DC, HEAD_DIM:]

    def prep(cg, carry):
        for g in range(DELTA_GROUP):
            prep_chunk(cg * DELTA_GROUP + g)
        return carry

    lax.fori_loop(0, nck // DELTA_GROUP, prep, 0)

    def chunk_update(S, c, dirn):
        v_new = u_s[dirn, c] - _bdot(w_s[dirn, c], S)
        o = _bdot(qd_s[dirn, c], S) + _bdot(qk_s[dirn, c], v_new)
        S = S * egl_s[dirn, c, 0:1, :] + _bdot(ktt_s[dirn, c], v_new)
        return S, o

    def step(i, carry):
        sf, sb = carry
        sf, o_f = chunk_update(sf, i, 0)
        of_s[pl.ds(pl.multiple_of(i * DC, DC), DC), :] = o_f
        cb = nck - 1 - i
        sb, o_b = chunk_update(sb, cb, 1)
        ob_s[pl.ds(pl.multiple_of(cb * DC, DC), DC), :] = o_b
        return sf, sb

    if use_s0:
        init = (s0_ref[0, 0, 0], s0_ref[0, 1, 0])
    else:
        init = (jnp.zeros((HEAD_DIM, HEAD_DIM), jnp.float32), jnp.zeros((HEAD_DIM, HEAD_DIM), jnp.float32))
    sf, sb = lax.fori_loop(0, nck, step, init)
    sfin_ref[0, 0, 0] = sf
    sfin_ref[0, 1, 0] = sb

    def fin(c, carry):
        rows = pl.ds(pl.multiple_of(c * DC, DC), DC)
        o = of_s[rows, :] + ob_s[rows, :]
        o = _rms(o) * og_ref[...]
        g = gate_ref[0, rows, :]
        o_ref[0, rows, :] = o * (g * _sigmoid(g))
        return carry

    lax.fori_loop(0, nck, fin, 0)


def delta_mixer_pallas(qkv, gate, ab, abt, cw, prow, pcol, onorm_g, s0):
    B, T, _ = qkv.shape
    H = A_HEADS
    state_block = (1, 2, 1, HEAD_DIM, HEAD_DIM)
    use_s0 = s0 is not None
    if use_s0:
        s0_spec = pl.BlockSpec(state_block, lambda b, h: (b, 0, h, 0, 0))
    else:
        s0 = jnp.zeros(state_block, jnp.float32)
        s0_spec = _full_spec(state_block)
    nck = T // DC
    seq = lambda off: pl.BlockSpec((1, T, HEAD_DIM), lambda b, h: (b, 0, off + h))
    assert nck % DELTA_GROUP == 0
    tok = pltpu.VMEM((T, HEAD_DIM), jnp.float32)
    blk = pltpu.VMEM((2, nck, DC, HEAD_DIM), jnp.float32)
    return pl.pallas_call(
        functools.partial(_delta_kernel, use_s0=use_s0),
        grid=(B, H),
        in_specs=[seq(0), seq(H), seq(2 * H), seq(0),
                  pl.BlockSpec((1, T, LANE), lambda b, h: (b, 0, 0)), pl.BlockSpec((AB_COLS, T), lambda b, h: (0, b)),
                  pl.BlockSpec((1, 3, A_CONV, HEAD_DIM), lambda b, h: (h, 0, 0, 0)),
                  _full_spec((2, LANE)), _full_spec((2, AB_COLS, LANE)), _full_spec((1, HEAD_DIM)), s0_spec],
        out_specs=[seq(0), pl.BlockSpec(state_block, lambda b, h: (b, 0, h, 0, 0))],
        out_shape=[jax.ShapeDtypeStruct((B, T, A_W), jnp.float32),
                   jax.ShapeDtypeStruct((B, 2, H, HEAD_DIM, HEAD_DIM), jnp.float32)],
        scratch_shapes=[blk, blk, blk, blk, blk, pltpu.VMEM((2, nck, 8, DC), jnp.float32), tok, tok],
        compiler_params=_cparams(2), name="delta",
    )(qkv, qkv, qkv, gate, ab, abt, cw, prow, pcol, onorm_g, s0)


def _softmax_segments(scores, values, sink):
    m = jnp.full((scores[0].shape[0], 1), sink, jnp.float32)
    for s in scores:
        m = jnp.maximum(m, jnp.max(s, axis=-1, keepdims=True))
    den = jnp.exp(sink - m)
    o = None
    for s, v in zip(scores, values):
        p = jnp.exp(s - m)
        den = den + jnp.sum(p, axis=-1, keepdims=True)
        pv = _bdot(p, v)
        o = pv if o is None else o + pv
    return o / den


def _ctx_attn_kernel(sink_ref, q_ref, k_ref, v_ref, o_ref):
    hkv = pl.program_id(1)
    k = k_ref[0]
    v = v_ref[0]
    for g in range(B_GROUP):
        q = q_ref[0, :, g * HEAD_DIM:(g + 1) * HEAD_DIM]
        s = _bdot_nt(q, k) * (HEAD_DIM ** -0.5)
        o_ref[0, :, g * HEAD_DIM:(g + 1) * HEAD_DIM] = _softmax_segments([s], [v], sink_ref[hkv * B_GROUP + g])


def context_attention_pallas(qb, kb, vb, sink):
    B, S, _ = qb.shape
    gw = B_GROUP * HEAD_DIM
    return pl.pallas_call(
        _ctx_attn_kernel,
        grid=(B, B_KV_HEADS),
        in_specs=[pl.BlockSpec(memory_space=pltpu.SMEM),
                  pl.BlockSpec((1, S, gw), lambda b, h: (b, 0, h)),
                  pl.BlockSpec((1, S, HEAD_DIM), lambda b, h: (b, 0, h)),
                  pl.BlockSpec((1, S, HEAD_DIM), lambda b, h: (b, 0, h))],
        out_specs=pl.BlockSpec((1, S, gw), lambda b, h: (b, 0, h)),
        out_shape=jax.ShapeDtypeStruct(qb.shape, jnp.float32),
        compiler_params=_cparams(2), name="ctx_attn",
    )(sink, qb, kb, vb)


def rope_tables(n_tokens):
    t = np.arange((n_tokens // GRID_W) * GRID_W)
    row = jnp.asarray((t // GRID_W).astype(np.float32))
    col = jnp.asarray((t % GRID_W).astype(np.float32))
    nf = HEAD_DIM // 4
    inv = ROPE_BASE ** (-jnp.arange(nf, dtype=jnp.float32) / nf)
    ar = row[:, None] * inv
    ac = col[:, None] * inv
    rc = jnp.concatenate([jnp.cos(ar), jnp.cos(ar), jnp.cos(ac), jnp.cos(ac)], axis=-1)
    rs = jnp.concatenate([-jnp.sin(ar), jnp.sin(ar), -jnp.sin(ac), jnp.sin(ac)], axis=-1)
    return rc, rs


def _rope(x, rc, rs):
    lane = lax.broadcasted_iota(jnp.int32, x.shape, 1)
    nf = HEAD_DIM // 4
    swapped = jnp.where((lane & (2 * nf - 1)) < nf, pltpu.roll(x, HEAD_DIM - nf, axis=1), pltpu.roll(x, nf, axis=1))
    return x * rc + swapped * rs


def _lat_attn_kernel(sink_ref, q_ref, kp_ref, kc_ref, kn_ref, vp_ref, vc_ref, vn_ref, kx_ref, vx_ref,
                     rcp_ref, rcc_ref, rcn_ref, rsp_ref, rsc_ref, rsn_ref, o_ref):
    hkv = pl.program_id(1)
    i = pl.program_id(2)
    nb = pl.num_programs(2)
    r = lax.broadcasted_iota(jnp.int32, (ATTN_BLOCK, ATTN_BLOCK), 0)
    c = lax.broadcasted_iota(jnp.int32, (ATTN_BLOCK, ATTN_BLOCK), 1)
    kp = _rope(kp_ref[0], rcp_ref[...], rsp_ref[...])
    kc = _rope(kc_ref[0], rcc_ref[...], rsc_ref[...])
    kn = _rope(kn_ref[0], rcn_ref[...], rsn_ref[...])
    ok_p = (c >= r) & (i > 0)
    ok_n = (c <= r) & (i < nb - 1)
    scale = HEAD_DIM ** -0.5
    for g in range(B_GROUP):
        q = _rope(q_ref[0, :, g * HEAD_DIM:(g + 1) * HEAD_DIM], rcc_ref[...], rsc_ref[...])
        sx = _bdot_nt(q, kx_ref[0]) * scale
        sp = jnp.where(ok_p, _bdot_nt(q, kp) * scale, NEG_INF)
        sc = _bdot_nt(q, kc) * scale
        sn = jnp.where(ok_n, _bdot_nt(q, kn) * scale, NEG_INF)
        o_ref[0, :, g * HEAD_DIM:(g + 1) * HEAD_DIM] = _softmax_segments(
            [sx, sp, sc, sn], [vx_ref[0], vp_ref[0], vc_ref[0], vn_ref[0]], sink_ref[hkv * B_GROUP + g])


def latent_attention_pallas(qb, kb, vb, k_ctx, v_ctx, sink):
    assert WINDOW == ATTN_BLOCK
    B, T, _ = qb.shape
    P = k_ctx.shape[1]
    nb = T // ATTN_BLOCK
    gw = B_GROUP * HEAD_DIM
    rc, rs = rope_tables(T)
    prev = lambda b, h, i: (b, jnp.maximum(i - 1, 0), h)
    cur = lambda b, h, i: (b, i, h)
    nxt = lambda b, h, i: (b, jnp.minimum(i + 1, nb - 1), h)
    blk = lambda f: pl.BlockSpec((1, ATTN_BLOCK, HEAD_DIM), f)
    tab = lambda f: pl.BlockSpec((ATTN_BLOCK, HEAD_DIM), lambda b, h, i: (f(b, h, i)[1], 0))
    ctx = pl.BlockSpec((1, P, HEAD_DIM), lambda b, h, i: (b, 0, h))
    return pl.pallas_call(
        _lat_attn_kernel,
        grid=(B, B_KV_HEADS, nb),
        in_specs=[pl.BlockSpec(memory_space=pltpu.SMEM), pl.BlockSpec((1, ATTN_BLOCK, gw), cur),
                  blk(prev), blk(cur), blk(nxt), blk(prev), blk(cur), blk(nxt), ctx, ctx,
                  tab(prev), tab(cur), tab(nxt), tab(prev), tab(cur), tab(nxt)],
        out_specs=pl.BlockSpec((1, ATTN_BLOCK, gw), cur),
        out_shape=jax.ShapeDtypeStruct(qb.shape, jnp.float32),
        compiler_params=_cparams(3), name="lat_attn",
    )(sink, qb, kb, kb, kb, vb, vb, vb, k_ctx, v_ctx, rc, rc, rc, rs, rs, rs)


def _top16(s, iota, iota16, want_rank):
    cur = s
    rank = jnp.full(s.shape, float(PEER_TOPK), jnp.float32) if want_rank else None
    sv = jnp.zeros((PEER_TOPK, s.shape[1]), jnp.float32)
    si = jnp.zeros((PEER_TOPK, s.shape[1]), jnp.float32)
    for k in range(PEER_TOPK):
        m = jnp.max(cur, axis=0, keepdims=True)
        idx = jnp.min(jnp.where(cur == m, iota, float(N_KEYS)), axis=0, keepdims=True)
        sel = iota == idx
        if want_rank:
            rank = jnp.where(sel, float(k), rank)
        else:
            si = jnp.where(iota16 == float(k), idx, si)
        cur = jnp.where(sel, -jnp.inf, cur)
        sv = jnp.where(iota16 == float(k), m, sv)
    return sv, (rank if want_rank else si)


def _top16_distinct(s, iota16):
    cur = s
    rank = jnp.full(s.shape, float(PEER_TOPK), jnp.float32)
    sv = jnp.zeros((PEER_TOPK, s.shape[1]), jnp.float32)
    for k in range(PEER_TOPK):
        m = jnp.max(cur, axis=0, keepdims=True)
        sel = cur == m
        rank = jnp.where(sel, float(k), rank)
        cur = jnp.where(sel, -jnp.inf, cur)
        sv = jnp.where(iota16 == float(k), m, sv)
    return sv, rank


def _staircase(sv0, sv1, iota16):
    n = jnp.zeros_like(sv0)
    f = sv0 + sv1[0:1, :]
    top = f[0:1, :]
    z = jnp.zeros_like(top)
    for _ in range(PEER_TOPK):
        fm = jnp.max(f, axis=0, keepdims=True)
        cand = iota16 * float(PEER_TOPK) + n
        ci = jnp.min(jnp.where(f == fm, cand, 1e9), axis=0, keepdims=True)
        win = cand == ci
        z = z + jnp.exp(fm - top)
        n = jnp.where(win, n + 1.0, n)
        nw = jnp.max(jnp.where(win, n, 0.0), axis=0, keepdims=True)
        nxt = jnp.max(jnp.where(iota16 == nw, sv1, -jnp.inf), axis=0, keepdims=True)
        f = jnp.where(win, sv0 + nxt, f)
    return n, z


def _peer_route_kernel(h_ref, wq_ref, keys_ref, ht_ref, r1_ref, p1_ref, c0_ref, q0_ref, q_scr, s_scr):
    tm = h_ref.shape[0]
    hf = h_ref[...]
    ht_ref[...] = hf.T.astype(jnp.bfloat16)
    q_scr[...] = jnp.dot(hf.astype(jnp.bfloat16), wq_ref[...], preferred_element_type=jnp.float32)

    def head_body(h, carry):
        for p in range(2):
            off = pl.multiple_of(h * D_KEY + p * (D_KEY // 2), LANE)
            s_scr[p] = _bdot_nt(keys_ref[p], q_scr[:, pl.ds(off, D_KEY // 2)])

        iota = lax.broadcasted_iota(jnp.int32, (N_KEYS, LANE), 0).astype(jnp.float32)
        iota16 = lax.broadcasted_iota(jnp.int32, (PEER_TOPK, LANE), 0).astype(jnp.float32)

        def emit(lanes, s0, s1, sv0, sv1, rank1, c0_terms):
            n, z = _staircase(sv0, sv1, iota16)
            c0 = jnp.zeros_like(s0)
            for k in range(PEER_TOPK):
                c0 = jnp.where(c0_terms(k), n[k:k + 1, :], c0)
            p1 = jnp.exp(s1 - sv1[0:1, :])
            r1_ref[h, :, lanes] = rank1
            p1_ref[h, :, lanes] = p1
            c0_ref[h, :, lanes] = c0
            q0_ref[h, :, lanes] = jnp.exp(s0 - sv0[0:1, :]) / z

        picked = None
        for c in range(tm // LANE):
            lanes = slice(c * LANE, (c + 1) * LANE)
            s0 = s_scr[0, :, lanes]
            s1 = s_scr[1, :, lanes]
            sv0, rank0 = _top16_distinct(s0, iota16)
            sv1, rank1 = _top16_distinct(s1, iota16)
            emit(lanes, s0, s1, sv0, sv1, rank1, lambda k: rank0 == float(k))
            cnt = (jnp.sum(jnp.where(rank0 < float(PEER_TOPK), 1.0, 0.0), axis=0, keepdims=True)
                   + jnp.sum(jnp.where(rank1 < float(PEER_TOPK), 1.0, 0.0), axis=0, keepdims=True))
            picked = cnt if picked is None else jnp.maximum(picked, cnt)

        @pl.when(jnp.max(picked) > float(2 * PEER_TOPK))
        def _():
            def exact_chunk(c, carry2):
                lanes = pl.ds(pl.multiple_of(c * LANE, LANE), LANE)
                s0 = s_scr[0, :, lanes]
                s1 = s_scr[1, :, lanes]
                sv0, si0 = _top16(s0, iota, iota16, False)
                sv1, rank1 = _top16(s1, iota, iota16, True)
                emit(lanes, s0, s1, sv0, sv1, rank1, lambda k: iota == si0[k:k + 1, :])
                return carry2

            lax.fori_loop(0, tm // LANE, exact_chunk, 0)

        return carry

    lax.fori_loop(0, PEER_HEADS, head_body, 0)


def _peer_expert_kernel(ht_ref, r1_ref, p1_ref, c0_ref, q0_ref, ed_ref, eut_ref, x_ref, gt_ref, fg_ref, out_ref,
                        acc_ref, act_ref, wg_ref, *, final_norm):
    e = pl.program_id(1)
    tm = ht_ref.shape[1]
    te = ed_ref.shape[0]

    @pl.when(e == 0)
    def _():
        acc_ref[...] = jnp.zeros_like(acc_ref)

    for sb in range(te // EXPERT_SUB):
        rs = slice(sb * EXPERT_SUB, (sb + 1) * EXPERT_SUB)
        act_ref[rs, :] = jnp.dot(ed_ref[rs, :], ht_ref[...], preferred_element_type=jnp.float32)
    for ii in range(te // N_KEYS):
        rows = slice(ii * N_KEYS, (ii + 1) * N_KEYS)
        for c in range(tm // LANE):
            lanes = slice(c * LANE, (c + 1) * LANE)
            g = None
            for h in range(PEER_HEADS):
                term = jnp.where(r1_ref[h, :, lanes] < c0_ref[h, ii:ii + 1, lanes], p1_ref[h, :, lanes], 0.0)
                term = term * q0_ref[h, ii:ii + 1, lanes]
                g = term if g is None else g + term
            wg_ref[rows, lanes] = (g * _gelu_tanh(act_ref[rows, lanes])).astype(jnp.bfloat16)
    for ub in range(te // EXPERT_UP):
        cols = slice(ub * EXPERT_UP, (ub + 1) * EXPERT_UP)
        acc_ref[...] += jnp.dot(eut_ref[:, cols], wg_ref[cols, :], preferred_element_type=jnp.float32)

    @pl.when(e == pl.num_programs(1) - 1)
    def _():
        y = x_ref[...] + gt_ref[0] * acc_ref[...].T
        if final_norm:
            y = _rms(y) * fg_ref[...]
        out_ref[...] = y


def peer_pallas(h, x, gate, wq_b, keys, ed_b, eut_b, final_g, final_norm, layer):
    T, D = h.shape
    nq = PEER_HEADS * D_KEY
    route_shape = jax.ShapeDtypeStruct((PEER_HEADS, N_KEYS, T), jnp.float32)
    tmr = ROUTE_TM
    route_spec = pl.BlockSpec((PEER_HEADS, N_KEYS, tmr), lambda i: (0, 0, i))
    ht, r1, p1, c0, q0 = pl.pallas_call(
        _peer_route_kernel,
        grid=(T // tmr,),
        in_specs=[_row_spec(tmr, D), pl.BlockSpec((None, D, nq), lambda i: (layer, 0, 0)),
                  pl.BlockSpec((None, 2, N_KEYS, D_KEY // 2), lambda i: (layer, 0, 0, 0))],
        out_specs=[pl.BlockSpec((D, tmr), lambda i: (0, i)), route_spec, route_spec, route_spec, route_spec],
        out_shape=[jax.ShapeDtypeStruct((D, T), jnp.bfloat16), route_shape, route_shape, route_shape, route_shape],
        scratch_shapes=[pltpu.VMEM((tmr, nq), jnp.float32), pltpu.VMEM((2, N_KEYS, tmr), jnp.float32)],
        compiler_params=_cparams(1), name="peer_route",
    )(h, wq_b, keys)

    tm, te = EXPERT_TM, EXPERT_TE
    full_spec = pl.BlockSpec((PEER_HEADS, N_KEYS, tm), lambda i, e: (0, 0, i))
    row_spec = pl.BlockSpec((PEER_HEADS, te // N_KEYS, tm), lambda i, e: (0, e, i))
    return pl.pallas_call(
        functools.partial(_peer_expert_kernel, final_norm=final_norm),
        grid=(T // tm, N_EXPERTS // te),
        in_specs=[pl.BlockSpec((D, tm), lambda i, e: (0, i)), full_spec, full_spec, row_spec, row_spec,
                  pl.BlockSpec((None, te, D), lambda i, e: (layer, e, 0)),
                  pl.BlockSpec((None, D, te), lambda i, e: (layer, 0, e)),
                  pl.BlockSpec((tm, D), lambda i, e: (i, 0)), _mod_spec(tm, T // gate.shape[0]), _full_spec((1, D))],
        out_specs=pl.BlockSpec((tm, D), lambda i, e: (i, 0)),
        out_shape=jax.ShapeDtypeStruct((T, D), jnp.float32),
        scratch_shapes=[pltpu.VMEM((D, tm), jnp.float32), pltpu.VMEM((te, tm), jnp.float32), pltpu.VMEM((te, tm), jnp.bfloat16)],
        compiler_params=_cparams(2), name="peer_expert",
    )(ht, r1, p1, c0, q0, ed_b, eut_b, x, gate, final_g.reshape(1, D))


def kernel(x_prompt, x_sample, cache_attn_k, cache_attn_v, state_delta, c, c_ctx, w_mod, b_mod, norm_g, w_in_ab, conv_ab, a_log_ab, dt_bias_ab, onorm_ab, sink_ab, w_out_ab, w_in_c, vnorm_c, w_s_c, b_s_c, w_out_c, peer_wq, peer_keys, peer_down, peer_up, final_norm_g):
    f32, bf16 = jnp.float32, jnp.bfloat16
    BP, SP, D = x_prompt.shape
    BS, TS, _ = x_sample.shape
    past = cache_attn_k.shape[2]
    H = A_HEADS
    cvecs = jnp.zeros((8, D), f32).at[0].set(c_ctx).at[1:1 + BS].set(c)
    streams = [dict(x=x_prompt.reshape(BP * SP, D), B=BP, T=SP, rows=slice(0, 1)),
               dict(x=x_sample.reshape(BS * TS, D), B=BS, T=TS, rows=slice(1, 1 + BS))]
    new_k = new_v = new_s = None
    wq_b = peer_wq.astype(bf16)
    ed_b = peer_down.astype(bf16)
    eut_b = jnp.swapaxes(peer_up, 1, 2).astype(bf16)
    for layer in range(DEPTH):
        m = modulation_pallas(cvecs, w_mod[layer], b_mod[layer])
        li = layer // 2
        last = layer == DEPTH - 1
        if layer % 2 == 0:
            w_in = w_in_ab[li]
            c0 = QKV_W + A_W
            w_main = jnp.concatenate([w_in[:, :c0], w_in[:, c0 + AB_COLS:]], axis=1).astype(bf16)
            w_log = w_in[:, c0:c0 + AB_COLS].astype(bf16)
            w_log_pad = jnp.pad(w_log, ((0, 0), (0, LANE - AB_COLS)))
            w_out_b = w_out_ab[li].astype(bf16)
            cw = conv_ab[li].reshape(A_CONV, 3, H, HEAD_DIM).transpose(2, 1, 0, 3)
            a_dec = jnp.exp(a_log_ab[li].astype(f32)).reshape(2 * H)
            dt = dt_bias_ab[li].astype(f32).reshape(2 * H)
            prow = jnp.zeros((2, LANE), f32).at[0, :2 * H].set(a_dec).at[1, :2 * H].set(dt)
            pcol = jnp.zeros((2, AB_COLS, LANE), f32).at[0, :2 * H].set(a_dec[:, None]).at[1, :2 * H].set(dt[:, None])
            og = onorm_ab[li].reshape(1, HEAD_DIM).astype(f32)
            sink = sink_ab[li].astype(f32)
        else:
            w_in_b = w_in_c[li].astype(bf16)
            w_s_b = w_s_c[li].astype(bf16)
            bs = jnp.pad(b_s_c[li].T.astype(f32), ((0, 0), (0, LANE - C_GROUPS)))
            w_out_b = w_out_c[li].astype(bf16)
        for si, st in enumerate(streams):
            B, T = st["B"], st["T"]
            mk = [m[st["rows"], k * D:(k + 1) * D].reshape(-1, 1, D) for k in range(6)]
            x = st["x"]
            if layer % 2 == 0:
                qkv, gate, qb, kb, vb, ab, abt = ab_in_pallas(x, mk[0], mk[1], norm_g[layer, 0], w_main, w_log_pad, w_log.T)
                s0 = None if si == 0 else state_delta[:, li]
                o_a, s_fin = delta_mixer_pallas(qkv.reshape(B, T, QKV_W), gate.reshape(B, T, A_W), ab.reshape(B, T, LANE),
                                                abt, cw, prow, pcol, og, s0)
                if si == 0:
                    o_b = context_attention_pallas(qb.reshape(B, T, QB_W), kb.reshape(B, T, KV_W), vb.reshape(B, T, KV_W), sink)
                    new_k = kb.reshape(B, 1, T, B_KV_HEADS, HEAD_DIM)
                    new_v = vb.reshape(B, 1, T, B_KV_HEADS, HEAD_DIM)
                    new_s = s_fin[:, None]
                else:
                    o_b = latent_attention_pallas(qb.reshape(B, T, QB_W), kb.reshape(B, T, KV_W), vb.reshape(B, T, KV_W),
                                                  cache_attn_k[:, li].reshape(B, past, KV_W),
                                                  cache_attn_v[:, li].reshape(B, past, KV_W), sink)
                x, hn = ab_out_pallas(o_a.reshape(B * T, A_W), o_b.reshape(B * T, QB_W), w_out_b, x, mk[2], mk[3], mk[4],
                                      norm_g[layer, 1])
            else:
                x, hn = c_layer_pallas(x, mk, norm_g[layer, 0], norm_g[layer, 1], w_in_b, vnorm_c[li], w_s_b, bs, w_out_b)
            st["x"] = peer_pallas(hn, x, mk[5], wq_b, peer_keys, ed_b, eut_b, final_norm_g, last, layer)
    y_prompt = streams[0]["x"].reshape(BP, SP, D)
    y_sample = streams[1]["x"].reshape(BS, TS, D)
    return (y_prompt, y_sample, new_k, new_v, new_s)
```

```python
import functools
import math

import numpy as np
import jax
import jax.numpy as jnp
from jax import lax
from jax.experimental import pallas as pl
from jax.experimental.pallas import tpu as pltpu

D_MODEL = 1024
DEPTH = 2
GRID_W = 64
HEAD_DIM = 128
EPS = 1e-6
NEG_INF = -1e30
A_HEADS = D_MODEL // (2 * HEAD_DIM)
A_W = A_HEADS * HEAD_DIM
A_CONV = 5
B_HEADS = D_MODEL // (2 * HEAD_DIM)
B_KV_HEADS = B_HEADS // 2
B_GROUP = B_HEADS // B_KV_HEADS
WINDOW = 128
ATTN_BLOCK = 128
ROPE_BASE = 10000.0
C_WIDTH = D_MODEL
C_GROUPS = 8
C_GROUP_W = C_WIDTH // C_GROUPS
C_CHUNK = 128
PEER_HEADS = 8
N_KEYS = 128
N_EXPERTS = N_KEYS * N_KEYS
PEER_TOPK = 16
D_KEY = 256

LANE = 128
VMEM_LIMIT = 48 * 1024 * 1024
PROJ_TM = 512
ROUTE_TM = 512
EXPERT_TM = 256
EXPERT_TE = 2048
EXPERT_SUB = 256
EXPERT_UP = 1024
MOD_TN = 1536
DC = 128

QKV_W = 3 * A_W
QB_W = B_HEADS * HEAD_DIM
KV_W = B_KV_HEADS * HEAD_DIM
AB_COLS = 4 * A_HEADS
AB_MAIN = QKV_W + A_W + QB_W + 2 * KV_W


def _bdot(a, b):
    return jnp.dot(a.astype(jnp.bfloat16), b.astype(jnp.bfloat16), preferred_element_type=jnp.float32)


def _bdot_nt(a, b):
    return lax.dot_general(a.astype(jnp.bfloat16), b.astype(jnp.bfloat16), (((1,), (1,)), ((), ())),
                           preferred_element_type=jnp.float32)


def _sigmoid(x):
    return 1.0 / (1.0 + jnp.exp(-x))


def _softplus(x):
    return jnp.maximum(x, 0.0) + jnp.log(1.0 + jnp.exp(-jnp.abs(x)))


def _gelu_tanh(x):
    k0 = -2.0 * math.sqrt(2.0 / math.pi) * math.log2(math.e)
    return x / (1.0 + jnp.exp2(x * (k0 + (k0 * 0.044715) * (x * x))))


def _rms(x):
    return x * lax.rsqrt(jnp.mean(x * x, axis=-1, keepdims=True) + EPS)


def _modnorm(x, g, shift, scale):
    return _rms(x) * g * (1.0 + scale) + shift


def _cparams(n):
    return pltpu.CompilerParams(dimension_semantics=("arbitrary",) * n, vmem_limit_bytes=VMEM_LIMIT)


def _mod_spec(tm, tokens_per_mod):
    return pl.BlockSpec((1, 1, D_MODEL), lambda i, *_: ((i * tm) // tokens_per_mod, 0, 0))


def _row_spec(tm, width):
    return pl.BlockSpec((tm, width), lambda i: (i, 0))


def _full_spec(shape):
    return pl.BlockSpec(shape, lambda *_: (0,) * len(shape))


def _mod_kernel(c_ref, w_ref, b_ref, o_ref):
    c = c_ref[...]
    o_ref[...] = _bdot(c * _sigmoid(c), w_ref[...]) + b_ref[...]


def modulation_pallas(cvecs, w_mod, b_mod):
    R, D = cvecs.shape
    N = w_mod.shape[1]
    return pl.pallas_call(
        _mod_kernel, grid=(N // MOD_TN,),
        in_specs=[pl.BlockSpec((R, D), lambda j: (0, 0)), pl.BlockSpec((D, MOD_TN), lambda j: (0, j)),
                  pl.BlockSpec((1, MOD_TN), lambda j: (0, j))],
        out_specs=pl.BlockSpec((R, MOD_TN), lambda j: (0, j)),
        out_shape=jax.ShapeDtypeStruct((R, N), jnp.float32),
        compiler_params=_cparams(1), name="modulation",
    )(cvecs, w_mod, b_mod.reshape(1, N))


def _ab_in_kernel(x_ref, sh_ref, sc_ref, g_ref, w_ref, wl_ref, wlt_ref,
                  qkv_ref, gate_ref, qb_ref, kb_ref, vb_ref, ab_ref, abt_ref):
    hb = _modnorm(x_ref[...], g_ref[...], sh_ref[0], sc_ref[0]).astype(jnp.bfloat16)
    res = jnp.dot(hb, w_ref[...], preferred_element_type=jnp.float32)
    o = 0
    for ref, wd in ((qkv_ref, QKV_W), (gate_ref, A_W), (qb_ref, QB_W), (kb_ref, KV_W), (vb_ref, KV_W)):
        ref[...] = res[:, o:o + wd]
        o += wd
    ab_ref[...] = jnp.dot(hb, wl_ref[...], preferred_element_type=jnp.float32)
    abt_ref[...] = lax.dot_general(wlt_ref[...], hb, (((1,), (1,)), ((), ())), preferred_element_type=jnp.float32)


def ab_in_pallas(x, shift, scale, g, w_main, w_log_pad, w_log_t):
    N, D = x.shape
    tm = PROJ_TM
    ms = _mod_spec(tm, N // shift.shape[0])
    outs = [(N, QKV_W), (N, A_W), (N, QB_W), (N, KV_W), (N, KV_W), (N, LANE)]
    return pl.pallas_call(
        _ab_in_kernel, grid=(N // tm,),
        in_specs=[_row_spec(tm, D), ms, ms, _full_spec((1, D)), _full_spec((D, AB_MAIN)), _full_spec((D, LANE)),
                  _full_spec((AB_COLS, D))],
        out_specs=[_row_spec(tm, s[1]) for s in outs] + [pl.BlockSpec((AB_COLS, tm), lambda i: (0, i))],
        out_shape=[jax.ShapeDtypeStruct(s, jnp.float32) for s in outs] + [jax.ShapeDtypeStruct((AB_COLS, N), jnp.float32)],
        compiler_params=_cparams(1), name="ab_in",
    )(x, shift, scale, g.reshape(1, D), w_main, w_log_pad, w_log_t)


def _ab_out_kernel(oa_ref, ob_ref, w_ref, x_ref, gt_ref, sh_ref, sc_ref, g_ref, xn_ref, hn_ref):
    res = _bdot(oa_ref[...], w_ref[0:A_W, :]) + _bdot(ob_ref[...], w_ref[A_W:, :])
    xn = x_ref[...] + gt_ref[0] * res
    xn_ref[...] = xn
    hn_ref[...] = _modnorm(xn, g_ref[...], sh_ref[0], sc_ref[0])


def ab_out_pallas(oa, ob, w_out_b, x, gate, shift, scale, g):
    N, D = x.shape
    tm = PROJ_TM
    ms = _mod_spec(tm, N // gate.shape[0])
    return pl.pallas_call(
        _ab_out_kernel, grid=(N // tm,),
        in_specs=[_row_spec(tm, A_W), _row_spec(tm, QB_W), _full_spec((A_W + QB_W, D)), _row_spec(tm, D), ms, ms, ms,
                  _full_spec((1, D))],
        out_specs=[_row_spec(tm, D), _row_spec(tm, D)],
        out_shape=[jax.ShapeDtypeStruct((N, D), jnp.float32)] * 2,
        compiler_params=_cparams(1), name="ab_out",
    )(oa, ob, w_out_b, x, gate, shift, scale, g.reshape(1, D))


def _c_layer_kernel(x_ref, sh1_ref, sc1_ref, g1_ref, wi_ref, vg_ref, ws_ref, bs_ref, wo_ref, gt_ref, sh2_ref, sc2_ref,
                    g2_ref, xn_ref, hn_ref, u_ref, vn_ref, z_ref):
    tm = x_ref.shape[0]
    x = x_ref[...]
    hb = _modnorm(x, g1_ref[...], sh1_ref[0], sc1_ref[0]).astype(jnp.bfloat16)
    res = _gelu_tanh(jnp.dot(hb, wi_ref[...], preferred_element_type=jnp.float32))
    u_ref[...] = res[:, :C_WIDTH]
    vn_ref[...] = _rms(res[:, C_WIDTH:]) * vg_ref[...]
    for cc in range(tm // C_CHUNK):
        rows = slice(cc * C_CHUNK, (cc + 1) * C_CHUNK)
        for grp in range(C_GROUPS):
            cols = slice(grp * C_GROUP_W, (grp + 1) * C_GROUP_W)
            mixed = _bdot(ws_ref[grp], vn_ref[rows, cols]) + bs_ref[:, grp:grp + 1]
            z_ref[rows, cols] = (u_ref[rows, cols] * mixed).astype(jnp.bfloat16)
    out = jnp.dot(z_ref[...], wo_ref[...], preferred_element_type=jnp.float32)
    xn = x + gt_ref[0] * out
    xn_ref[...] = xn
    hn_ref[...] = _modnorm(xn, g2_ref[...], sh2_ref[0], sc2_ref[0])


def c_layer_pallas(x, mk, g1, g2, w_in_b, vnorm_g, w_s_b, bs, w_out_b):
    N, D = x.shape
    tm = PROJ_TM
    ms = _mod_spec(tm, N // mk[0].shape[0])
    return pl.pallas_call(
        _c_layer_kernel, grid=(N // tm,),
        in_specs=[_row_spec(tm, D), ms, ms, _full_spec((1, D)), _full_spec((D, 2 * C_WIDTH)), _full_spec((1, C_WIDTH)),
                  _full_spec((C_GROUPS, C_CHUNK, C_CHUNK)), _full_spec((C_CHUNK, LANE)), _full_spec((C_WIDTH, D)),
                  ms, ms, ms, _full_spec((1, D))],
        out_specs=[_row_spec(tm, D), _row_spec(tm, D)],
        out_shape=[jax.ShapeDtypeStruct((N, D), jnp.float32)] * 2,
        scratch_shapes=[pltpu.VMEM((tm, C_WIDTH), jnp.float32), pltpu.VMEM((tm, C_WIDTH), jnp.float32),
                        pltpu.VMEM((tm, C_WIDTH), jnp.bfloat16)],
        compiler_params=_cparams(1), name="c_layer",
    )(x, mk[0], mk[1], g1.reshape(1, D), w_in_b, vnorm_g.reshape(1, C_WIDTH), w_s_b, bs, w_out_b,
      mk[2], mk[3], mk[4], g2.reshape(1, D))


def _split3(x):
    hi = x.astype(jnp.bfloat16)
    r = x - hi.astype(jnp.float32)
    mid = r.astype(jnp.bfloat16)
    lo = (r - mid.astype(jnp.float32)).astype(jnp.bfloat16)
    return hi, mid, lo


def _unit_tri_inverse_offdiag(a, ii, jj):
    def same_block(b):
        return jnp.right_shift(ii, b) == jnp.right_shift(jj, b)
    n = -jnp.where(same_block(1), a, 0.0)
    for b in range(1, 7):
        o = jnp.where(same_block(b + 1) & jnp.logical_not(same_block(b)), a, 0.0)
        p = o + _bdot(n, o)
        q = p + _bdot(p, n)
        n = n - q
    return n


DELTA_GROUP = 2


def _delta_kernel(q_ref, k_ref, v_ref, gate_ref, ab_ref, abt_ref, cw_ref, prow_ref, pcol_ref, og_ref, s0_ref,
                  o_ref, sfin_ref, u_s, w_s, qk_s, qd_s, ktt_s, egl_s, of_s, ob_s, *, use_s0):
    T = q_ref.shape[1]
    nck = T // DC
    h = pl.program_id(1)
    ii = lax.broadcasted_iota(jnp.int32, (DC, DC), 0)
    jj = lax.broadcasted_iota(jnp.int32, (DC, DC), 1)
    low = (jj <= ii).astype(jnp.bfloat16)
    upp = (jj >= ii).astype(jnp.bfloat16)
    tri_cols = jnp.concatenate([low, upp], axis=0)
    tri_rows = jnp.concatenate([upp, low], axis=1)
    r16 = lax.broadcasted_iota(jnp.int32, (AB_COLS, DC), 0)
    i2 = lax.broadcasted_iota(jnp.int32, (2 * DC, 2 * DC), 0)
    j2 = lax.broadcasted_iota(jnp.int32, (2 * DC, 2 * DC), 1)

    def shifted(ref, c, d):
        cur = ref[0, pl.ds(pl.multiple_of(c * DC, DC), DC), :]
        if d == 0:
            return cur
        step = 1 if d > 0 else -1
        cn = jnp.clip(c + step, 0, nck - 1)
        nb = ref[0, pl.ds(pl.multiple_of(cn * DC, DC), DC), :]
        ok = jnp.where((c + step >= 0) & (c + step < nck), 1.0, 0.0)
        rc = pltpu.roll(cur, (-d) % DC, axis=0)
        rn = pltpu.roll(nb, (-d) % DC, axis=0) * ok
        return jnp.where((ii + d >= 0) & (ii + d < DC), rc, rn)

    def prep_chunk(c):
        rows = pl.ds(pl.multiple_of(c * DC, DC), DC)

        def conv_silu(ref, idx):
            acc = jnp.zeros((DC, HEAD_DIM), jnp.float32)
            for kk in range(A_CONV):
                acc = acc + shifted(ref, c, kk - A_CONV // 2) * cw_ref[0, idx, kk:kk + 1, :]
            return acc * _sigmoid(acc)

        q = conv_silu(q_ref, 0)
        k = conv_silu(k_ref, 1)
        v = conv_silu(v_ref, 2)
        q = q * lax.rsqrt(jnp.sum(q * q, axis=-1, keepdims=True) + EPS) * (HEAD_DIM ** -0.5)
        k = k * lax.rsqrt(jnp.sum(k * k, axis=-1, keepdims=True) + EPS)
        kt = k.T
        ab = ab_ref[0, rows, :]
        g_all = -prow_ref[0:1, :] * _softplus(ab + prow_ref[1:2, :])
        beta_all = _sigmoid(ab)
        abt = abt_ref[:, rows]
        gr_all = -pcol_ref[0] * _softplus(abt + pcol_ref[1])

        def pick_col(x, col):
            return jnp.sum(jnp.where(jj == col, x, 0.0), axis=1, keepdims=True)

        def pick_row(x, row):
            return jnp.sum(jnp.where(r16 == row, x, 0.0), axis=0, keepdims=True)

        qkt = _bdot(q, kt)
        f32 = jnp.float32
        gcs = sum(jnp.dot(tri_cols, t, preferred_element_type=f32) for t in _split3(g_all))
        grs = sum(jnp.dot(t, tri_rows, preferred_element_type=f32) for t in _split3(gr_all))
        per_dir = []
        for dirn in range(2):
            col = dirn * A_HEADS + h
            gc = pick_col(gcs[dirn * DC:(dirn + 1) * DC], col)
            gr = pick_row(grs[:, dirn * DC:(dirn + 1) * DC], col)
            beta = pick_col(beta_all, 2 * A_HEADS + col)
            tri = (jj <= ii) if dirn == 0 else (jj >= ii)
            gamma = jnp.where(tri, jnp.exp(jnp.where(tri, gc - gr, 0.0)), 0.0)
            kb = k * beta
            egc = jnp.exp(gc)
            gl = gc[DC - 1:DC, :] if dirn == 0 else gc[0:1, :]
            qk_s[dirn, c] = qkt * gamma
            qd_s[dirn, c] = q * egc
            ktt_s[dirn, c] = kt * jnp.exp(gl - gr)
            egl_s[dirn, c] = jnp.broadcast_to(jnp.exp(gl), (8, DC))
            per_dir.append((kb, gamma, jnp.concatenate([v * beta, kb * egc], axis=1)))
        kk = _bdot(jnp.concatenate([per_dir[0][0], per_dir[1][0]], axis=0), kt)
        zero = jnp.zeros((DC, DC), jnp.float32)
        a_f = jnp.where(jj < ii, kk[:DC] * per_dir[0][1], 0.0)
        a_b = jnp.where(jj > ii, kk[DC:] * per_dir[1][1], 0.0)
        a = jnp.concatenate([jnp.concatenate([a_f, zero], axis=1), jnp.concatenate([zero, a_b], axis=1)], axis=0)
        n = _unit_tri_inverse_offdiag(a, i2, j2)
        rhs = jnp.concatenate([per_dir[0][2], per_dir[1][2]], axis=0)
        sol = rhs + _bdot(n, rhs)
        for dirn in range(2):
            u_s[dirn, c] = sol[dirn * DC:(dirn + 1) * DC, :HEAD_DIM]
            w_s[dirn, c] = sol[dirn * DC:(dirn + 1) * DC, HEAD_DIM:]

    def prep(cg, carry):
        for g in range(DELTA_GROUP):
            prep_chunk(cg * DELTA_GROUP + g)
        return carry

    lax.fori_loop(0, nck // DELTA_GROUP, prep, 0)

    def chunk_update(S, c, dirn):
        v_new = u_s[dirn, c] - _bdot(w_s[dirn, c], S)
        o = _bdot(qd_s[dirn, c], S) + _bdot(qk_s[dirn, c], v_new)
        S = S * egl_s[dirn, c, 0:1, :] + _bdot(ktt_s[dirn, c], v_new)
        return S, o

    def step(i, carry):
        sf, sb = carry
        sf, o_f = chunk_update(sf, i, 0)
        of_s[pl.ds(pl.multiple_of(i * DC, DC), DC), :] = o_f
        cb = nck - 1 - i
        sb, o_b = chunk_update(sb, cb, 1)
        ob_s[pl.ds(pl.multiple_of(cb * DC, DC), DC), :] = o_b
        return sf, sb

    if use_s0:
        init = (s0_ref[0, 0, 0], s0_ref[0, 1, 0])
    else:
        init = (jnp.zeros((HEAD_DIM, HEAD_DIM), jnp.float32), jnp.zeros((HEAD_DIM, HEAD_DIM), jnp.float32))
    sf, sb = lax.fori_loop(0, nck, step, init)
    sfin_ref[0, 0, 0] = sf
    sfin_ref[0, 1, 0] = sb

    def fin(c, carry):
        rows = pl.ds(pl.multiple_of(c * DC, DC), DC)
        o = of_s[rows, :] + ob_s[rows, :]
        o = _rms(o) * og_ref[...]
        g = gate_ref[0, rows, :]
        o_ref[0, rows, :] = o * (g * _sigmoid(g))
        return carry

    lax.fori_loop(0, nck, fin, 0)


def delta_mixer_pallas(qkv, gate, ab, abt, cw, prow, pcol, onorm_g, s0):
    B, T, _ = qkv.shape
    H = A_HEADS
    state_block = (1, 2, 1, HEAD_DIM, HEAD_DIM)
    use_s0 = s0 is not None
    if use_s0:
        s0_spec = pl.BlockSpec(state_block, lambda b, h: (b, 0, h, 0, 0))
    else:
        s0 = jnp.zeros(state_block, jnp.float32)
        s0_spec = _full_spec(state_block)
    nck = T // DC
    seq = lambda off: pl.BlockSpec((1, T, HEAD_DIM), lambda b, h: (b, 0, off + h))
    assert nck % DELTA_GROUP == 0
    tok = pltpu.VMEM((T, HEAD_DIM), jnp.float32)
    blk = pltpu.VMEM((2, nck, DC, HEAD_DIM), jnp.float32)
    return pl.pallas_call(
        functools.partial(_delta_kernel, use_s0=use_s0),
        grid=(B, H),
        in_specs=[seq(0), seq(H), seq(2 * H), seq(0),
                  pl.BlockSpec((1, T, LANE), lambda b, h: (b, 0, 0)), pl.BlockSpec((AB_COLS, T), lambda b, h: (0, b)),
                  pl.BlockSpec((1, 3, A_CONV, HEAD_DIM), lambda b, h: (h, 0, 0, 0)),
                  _full_spec((2, LANE)), _full_spec((2, AB_COLS, LANE)), _full_spec((1, HEAD_DIM)), s0_spec],
        out_specs=[seq(0), pl.BlockSpec(state_block, lambda b, h: (b, 0, h, 0, 0))],
        out_shape=[jax.ShapeDtypeStruct((B, T, A_W), jnp.float32),
                   jax.ShapeDtypeStruct((B, 2, H, HEAD_DIM, HEAD_DIM), jnp.float32)],
        scratch_shapes=[blk, blk, blk, blk, blk, pltpu.VMEM((2, nck, 8, DC), jnp.float32), tok, tok],
        compiler_params=_cparams(2), name="delta",
    )(qkv, qkv, qkv, gate, ab, abt, cw, prow, pcol, onorm_g, s0)


def _softmax_segments(scores, values, sink):
    m = jnp.full((scores[0].shape[0], 1), sink, jnp.float32)
    for s in scores:
        m = jnp.maximum(m, jnp.max(s, axis=-1, keepdims=True))
    den = jnp.exp(sink - m)
    o = None
    for s, v in zip(scores, values):
        p = jnp.exp(s - m)
        den = den + jnp.sum(p, axis=-1, keepdims=True)
        pv = _bdot(p, v)
        o = pv if o is None else o + pv
    return o / den


def _ctx_attn_kernel(sink_ref, q_ref, k_ref, v_ref, o_ref):
    hkv = pl.program_id(1)
    k = k_ref[0]
    v = v_ref[0]
    for g in range(B_GROUP):
        q = q_ref[0, :, g * HEAD_DIM:(g + 1) * HEAD_DIM]
        s = _bdot_nt(q, k) * (HEAD_DIM ** -0.5)
        o_ref[0, :, g * HEAD_DIM:(g + 1) * HEAD_DIM] = _softmax_segments([s], [v], sink_ref[hkv * B_GROUP + g])


def context_attention_pallas(qb, kb, vb, sink):
    B, S, _ = qb.shape
    gw = B_GROUP * HEAD_DIM
    return pl.pallas_call(
        _ctx_attn_kernel,
        grid=(B, B_KV_HEADS),
        in_specs=[pl.BlockSpec(memory_space=pltpu.SMEM),
                  pl.BlockSpec((1, S, gw), lambda b, h: (b, 0, h)),
                  pl.BlockSpec((1, S, HEAD_DIM), lambda b, h: (b, 0, h)),
                  pl.BlockSpec((1, S, HEAD_DIM), lambda b, h: (b, 0, h))],
        out_specs=pl.BlockSpec((1, S, gw), lambda b, h: (b, 0, h)),
        out_shape=jax.ShapeDtypeStruct(qb.shape, jnp.float32),
        compiler_params=_cparams(2), name="ctx_attn",
    )(sink, qb, kb, vb)


def rope_tables(n_tokens):
    t = np.arange((n_tokens // GRID_W) * GRID_W)
    row = jnp.asarray((t // GRID_W).astype(np.float32))
    col = jnp.asarray((t % GRID_W).astype(np.float32))
    nf = HEAD_DIM // 4
    inv = ROPE_BASE ** (-jnp.arange(nf, dtype=jnp.float32) / nf)
    ar = row[:, None] * inv
    ac = col[:, None] * inv
    rc = jnp.concatenate([jnp.cos(ar), jnp.cos(ar), jnp.cos(ac), jnp.cos(ac)], axis=-1)
    rs = jnp.concatenate([-jnp.sin(ar), jnp.sin(ar), -jnp.sin(ac), jnp.sin(ac)], axis=-1)
    return rc, rs


def _rope(x, rc, rs):
    lane = lax.broadcasted_iota(jnp.int32, x.shape, 1)
    nf = HEAD_DIM // 4
    swapped = jnp.where((lane & (2 * nf - 1)) < nf, pltpu.roll(x, HEAD_DIM - nf, axis=1), pltpu.roll(x, nf, axis=1))
    return x * rc + swapped * rs


def _lat_attn_kernel(sink_ref, q_ref, kp_ref, kc_ref, kn_ref, vp_ref, vc_ref, vn_ref, kx_ref, vx_ref,
                     rcp_ref, rcc_ref, rcn_ref, rsp_ref, rsc_ref, rsn_ref, o_ref):
    hkv = pl.program_id(1)
    i = pl.program_id(2)
    nb = pl.num_programs(2)
    r = lax.broadcasted_iota(jnp.int32, (ATTN_BLOCK, ATTN_BLOCK), 0)
    c = lax.broadcasted_iota(jnp.int32, (ATTN_BLOCK, ATTN_BLOCK), 1)
    kp = _rope(kp_ref[0], rcp_ref[...], rsp_ref[...])
    kc = _rope(kc_ref[0], rcc_ref[...], rsc_ref[...])
    kn = _rope(kn_ref[0], rcn_ref[...], rsn_ref[...])
    ok_p = (c >= r) & (i > 0)
    ok_n = (c <= r) & (i < nb - 1)
    scale = HEAD_DIM ** -0.5
    for g in range(B_GROUP):
        q = _rope(q_ref[0, :, g * HEAD_DIM:(g + 1) * HEAD_DIM], rcc_ref[...], rsc_ref[...])
        sx = _bdot_nt(q, kx_ref[0]) * scale
        sp = jnp.where(ok_p, _bdot_nt(q, kp) * scale, NEG_INF)
        sc = _bdot_nt(q, kc) * scale
        sn = jnp.where(ok_n, _bdot_nt(q, kn) * scale, NEG_INF)
        o_ref[0, :, g * HEAD_DIM:(g + 1) * HEAD_DIM] = _softmax_segments(
            [sx, sp, sc, sn], [vx_ref[0], vp_ref[0], vc_ref[0], vn_ref[0]], sink_ref[hkv * B_GROUP + g])


def latent_attention_pallas(qb, kb, vb, k_ctx, v_ctx, sink):
    assert WINDOW == ATTN_BLOCK
    B, T, _ = qb.shape
    P = k_ctx.shape[1]
    nb = T // ATTN_BLOCK
    gw = B_GROUP * HEAD_DIM
    rc, rs = rope_tables(T)
    prev = lambda b, h, i: (b, jnp.maximum(i - 1, 0), h)
    cur = lambda b, h, i: (b, i, h)
    nxt = lambda b, h, i: (b, jnp.minimum(i + 1, nb - 1), h)
    blk = lambda f: pl.BlockSpec((1, ATTN_BLOCK, HEAD_DIM), f)
    tab = lambda f: pl.BlockSpec((ATTN_BLOCK, HEAD_DIM), lambda b, h, i: (f(b, h, i)[1], 0))
    ctx = pl.BlockSpec((1, P, HEAD_DIM), lambda b, h, i: (b, 0, h))
    return pl.pallas_call(
        _lat_attn_kernel,
        grid=(B, B_KV_HEADS, nb),
        in_specs=[pl.BlockSpec(memory_space=pltpu.SMEM), pl.BlockSpec((1, ATTN_BLOCK, gw), cur),
                  blk(prev), blk(cur), blk(nxt), blk(prev), blk(cur), blk(nxt), ctx, ctx,
                  tab(prev), tab(cur), tab(nxt), tab(prev), tab(cur), tab(nxt)],
        out_specs=pl.BlockSpec((1, ATTN_BLOCK, gw), cur),
        out_shape=jax.ShapeDtypeStruct(qb.shape, jnp.float32),
        compiler_params=_cparams(3), name="lat_attn",
    )(sink, qb, kb, kb, kb, vb, vb, vb, k_ctx, v_ctx, rc, rc, rc, rs, rs, rs)


def _top16(s, iota, iota16, want_rank):
    cur = s
    rank = jnp.full(s.shape, float(PEER_TOPK), jnp.float32) if want_rank else None
    sv = jnp.zeros((PEER_TOPK, s.shape[1]), jnp.float32)
    si = jnp.zeros((PEER_TOPK, s.shape[1]), jnp.float32)
    for k in range(PEER_TOPK):
        m = jnp.max(cur, axis=0, keepdims=True)
        idx = jnp.min(jnp.where(cur == m, iota, float(N_KEYS)), axis=0, keepdims=True)
        sel = iota == idx
        if want_rank:
            rank = jnp.where(sel, float(k), rank)
        else:
            si = jnp.where(iota16 == float(k), idx, si)
        cur = jnp.where(sel, -jnp.inf, cur)
        sv = jnp.where(iota16 == float(k), m, sv)
    return sv, (rank if want_rank else si)


def _top16_distinct(s, iota16):
    cur = s
    rank = jnp.full(s.shape, float(PEER_TOPK), jnp.float32)
    sv = jnp.zeros((PEER_TOPK, s.shape[1]), jnp.float32)
    for k in range(PEER_TOPK):
        m = jnp.max(cur, axis=0, keepdims=True)
        sel = cur == m
        rank = jnp.where(sel, float(k), rank)
        cur = jnp.where(sel, -jnp.inf, cur)
        sv = jnp.where(iota16 == float(k), m, sv)
    return sv, rank


def _staircase(sv0, sv1, iota16):
    n = jnp.zeros_like(sv0)
    f = sv0 + sv1[0:1, :]
    top = f[0:1, :]
    z = jnp.zeros_like(top)
    for _ in range(PEER_TOPK):
        fm = jnp.max(f, axis=0, keepdims=True)
        cand = iota16 * float(PEER_TOPK) + n
        ci = jnp.min(jnp.where(f == fm, cand, 1e9), axis=0, keepdims=True)
        win = cand == ci
        z = z + jnp.exp(fm - top)
        n = jnp.where(win, n + 1.0, n)
        nw = jnp.max(jnp.where(win, n, 0.0), axis=0, keepdims=True)
        nxt = jnp.max(jnp.where(iota16 == nw, sv1, -jnp.inf), axis=0, keepdims=True)
        f = jnp.where(win, sv0 + nxt, f)
    return n, z


def _peer_route_kernel(h_ref, wq_ref, keys_ref, ht_ref, r1_ref, p1_ref, c0_ref, q0_ref, q_scr, s_scr):
    tm = h_ref.shape[0]
    hf = h_ref[...]
    ht_ref[...] = hf.T.astype(jnp.bfloat16)
    q_scr[...] = jnp.dot(hf.astype(jnp.bfloat16), wq_ref[...], preferred_element_type=jnp.float32)

    def head_body(h, carry):
        for p in range(2):
            off = pl.multiple_of(h * D_KEY + p * (D_KEY // 2), LANE)
            s_scr[p] = _bdot_nt(keys_ref[p], q_scr[:, pl.ds(off, D_KEY // 2)])

        iota = lax.broadcasted_iota(jnp.int32, (N_KEYS, LANE), 0).astype(jnp.float32)
        iota16 = lax.broadcasted_iota(jnp.int32, (PEER_TOPK, LANE), 0).astype(jnp.float32)

        def emit(lanes, s0, s1, sv0, sv1, rank1, c0_terms):
            n, z = _staircase(sv0, sv1, iota16)
            c0 = jnp.zeros_like(s0)
            for k in range(PEER_TOPK):
                c0 = jnp.where(c0_terms(k), n[k:k + 1, :], c0)
            p1 = jnp.exp(s1 - sv1[0:1, :])
            r1_ref[h, :, lanes] = rank1
            p1_ref[h, :, lanes] = p1
            c0_ref[h, :, lanes] = c0
            q0_ref[h, :, lanes] = jnp.exp(s0 - sv0[0:1, :]) / z

        picked = None
        for c in range(tm // LANE):
            lanes = slice(c * LANE, (c + 1) * LANE)
            s0 = s_scr[0, :, lanes]
            s1 = s_scr[1, :, lanes]
            sv0, rank0 = _top16_distinct(s0, iota16)
            sv1, rank1 = _top16_distinct(s1, iota16)
            emit(lanes, s0, s1, sv0, sv1, rank1, lambda k: rank0 == float(k))
            cnt = (jnp.sum(jnp.where(rank0 < float(PEER_TOPK), 1.0, 0.0), axis=0, keepdims=True)
                   + jnp.sum(jnp.where(rank1 < float(PEER_TOPK), 1.0, 0.0), axis=0, keepdims=True))
            picked = cnt if picked is None else jnp.maximum(picked, cnt)

        @pl.when(jnp.max(picked) > float(2 * PEER_TOPK))
        def _():
            def exact_chunk(c, carry2):
                lanes = pl.ds(pl.multiple_of(c * LANE, LANE), LANE)
                s0 = s_scr[0, :, lanes]
                s1 = s_scr[1, :, lanes]
                sv0, si0 = _top16(s0, iota, iota16, False)
                sv1, rank1 = _top16(s1, iota, iota16, True)
                emit(lanes, s0, s1, sv0, sv1, rank1, lambda k: iota == si0[k:k + 1, :])
                return carry2

            lax.fori_loop(0, tm // LANE, exact_chunk, 0)

        return carry

    lax.fori_loop(0, PEER_HEADS, head_body, 0)


def _peer_expert_kernel(ht_ref, r1_ref, p1_ref, c0_ref, q0_ref, ed_ref, eut_ref, x_ref, gt_ref, fg_ref, out_ref,
                        acc_ref, act_ref, wg_ref, *, final_norm):
    e = pl.program_id(1)
    tm = ht_ref.shape[1]
    te = ed_ref.shape[0]

    @pl.when(e == 0)
    def _():
        acc_ref[...] = jnp.zeros_like(acc_ref)

    for sb in range(te // EXPERT_SUB):
        rs = slice(sb * EXPERT_SUB, (sb + 1) * EXPERT_SUB)
        act_ref[rs, :] = jnp.dot(ed_ref[rs, :], ht_ref[...], preferred_element_type=jnp.float32)
    for ii in range(te // N_KEYS):
        rows = slice(ii * N_KEYS, (ii + 1) * N_KEYS)
        for c in range(tm // LANE):
            lanes = slice(c * LANE, (c + 1) * LANE)
            g = None
            for h in range(PEER_HEADS):
                term = jnp.where(r1_ref[h, :, lanes] < c0_ref[h, ii:ii + 1, lanes], p1_ref[h, :, lanes], 0.0)
                term = term * q0_ref[h, ii:ii + 1, lanes]
                g = term if g is None else g + term
            wg_ref[rows, lanes] = (g * _gelu_tanh(act_ref[rows, lanes])).astype(jnp.bfloat16)
    for ub in range(te // EXPERT_UP):
        cols = slice(ub * EXPERT_UP, (ub + 1) * EXPERT_UP)
        acc_ref[...] += jnp.dot(eut_ref[:, cols], wg_ref[cols, :], preferred_element_type=jnp.float32)

    @pl.when(e == pl.num_programs(1) - 1)
    def _():
        y = x_ref[...] + gt_ref[0] * acc_ref[...].T
        if final_norm:
            y = _rms(y) * fg_ref[...]
        out_ref[...] = y


def peer_pallas(h, x, gate, wq_b, keys, ed_b, eut_b, final_g, final_norm, layer):
    T, D = h.shape
    nq = PEER_HEADS * D_KEY
    route_shape = jax.ShapeDtypeStruct((PEER_HEADS, N_KEYS, T), jnp.float32)
    tmr = ROUTE_TM
    route_spec = pl.BlockSpec((PEER_HEADS, N_KEYS, tmr), lambda i: (0, 0, i))
    ht, r1, p1, c0, q0 = pl.pallas_call(
        _peer_route_kernel,
        grid=(T // tmr,),
        in_specs=[_row_spec(tmr, D), pl.BlockSpec((None, D, nq), lambda i: (layer, 0, 0)),
                  pl.BlockSpec((None, 2, N_KEYS, D_KEY // 2), lambda i: (layer, 0, 0, 0))],
        out_specs=[pl.BlockSpec((D, tmr), lambda i: (0, i)), route_spec, route_spec, route_spec, route_spec],
        out_shape=[jax.ShapeDtypeStruct((D, T), jnp.bfloat16), route_shape, route_shape, route_shape, route_shape],
        scratch_shapes=[pltpu.VMEM((tmr, nq), jnp.float32), pltpu.VMEM((2, N_KEYS, tmr), jnp.float32)],
        compiler_params=_cparams(1), name="peer_route",
    )(h, wq_b, keys)

    tm, te = EXPERT_TM, EXPERT_TE
    full_spec = pl.BlockSpec((PEER_HEADS, N_KEYS, tm), lambda i, e: (0, 0, i))
    row_spec = pl.BlockSpec((PEER_HEADS, te // N_KEYS, tm), lambda i, e: (0, e, i))
    return pl.pallas_call(
        functools.partial(_peer_expert_kernel, final_norm=final_norm),
        grid=(T // tm, N_EXPERTS // te),
        in_specs=[pl.BlockSpec((D, tm), lambda i, e: (0, i)), full_spec, full_spec, row_spec, row_spec,
                  pl.BlockSpec((None, te, D), lambda i, e: (layer, e, 0)),
                  pl.BlockSpec((None, D, te), lambda i, e: (layer, 0, e)),
                  pl.BlockSpec((tm, D), lambda i, e: (i, 0)), _mod_spec(tm, T // gate.shape[0]), _full_spec((1, D))],
        out_specs=pl.BlockSpec((tm, D), lambda i, e: (i, 0)),
        out_shape=jax.ShapeDtypeStruct((T, D), jnp.float32),
        scratch_shapes=[pltpu.VMEM((D, tm), jnp.float32), pltpu.VMEM((te, tm), jnp.float32), pltpu.VMEM((te, tm), jnp.bfloat16)],
        compiler_params=_cparams(2), name="peer_expert",
    )(ht, r1, p1, c0, q0, ed_b, eut_b, x, gate, final_g.reshape(1, D))


def kernel(x_prompt, x_sample, cache_attn_k, cache_attn_v, state_delta, c, c_ctx, w_mod, b_mod, norm_g, w_in_ab, conv_ab, a_log_ab, dt_bias_ab, onorm_ab, sink_ab, w_out_ab, w_in_c, vnorm_c, w_s_c, b_s_c, w_out_c, peer_wq, peer_keys, peer_down, peer_up, final_norm_g):
    f32, bf16 = jnp.float32, jnp.bfloat16
    BP, SP, D = x_prompt.shape
    BS, TS, _ = x_sample.shape
    past = cache_attn_k.shape[2]
    H = A_HEADS
    cvecs = jnp.zeros((8, D), f32).at[0].set(c_ctx).at[1:1 + BS].set(c)
    streams = [dict(x=x_prompt.reshape(BP * SP, D), B=BP, T=SP, rows=slice(0, 1)),
               dict(x=x_sample.reshape(BS * TS, D), B=BS, T=TS, rows=slice(1, 1 + BS))]
    new_k = new_v = new_s = None
    wq_b = peer_wq.astype(bf16)
    ed_b = peer_down.astype(bf16)
    eut_b = jnp.swapaxes(peer_up, 1, 2).astype(bf16)
    for layer in range(DEPTH):
        m = modulation_pallas(cvecs, w_mod[layer], b_mod[layer])
        li = layer // 2
        last = layer == DEPTH - 1
        if layer % 2 == 0:
            w_in = w_in_ab[li]
            c0 = QKV_W + A_W
            w_main = jnp.concatenate([w_in[:, :c0], w_in[:, c0 + AB_COLS:]], axis=1).astype(bf16)
            w_log = w_in[:, c0:c0 + AB_COLS].astype(bf16)
            w_log_pad = jnp.pad(w_log, ((0, 0), (0, LANE - AB_COLS)))
            w_out_b = w_out_ab[li].astype(bf16)
            cw = conv_ab[li].reshape(A_CONV, 3, H, HEAD_DIM).transpose(2, 1, 0, 3)
            a_dec = jnp.exp(a_log_ab[li].astype(f32)).reshape(2 * H)
            dt = dt_bias_ab[li].astype(f32).reshape(2 * H)
            prow = jnp.zeros((2, LANE), f32).at[0, :2 * H].set(a_dec).at[1, :2 * H].set(dt)
            pcol = jnp.zeros((2, AB_COLS, LANE), f32).at[0, :2 * H].set(a_dec[:, None]).at[1, :2 * H].set(dt[:, None])
            og = onorm_ab[li].reshape(1, HEAD_DIM).astype(f32)
            sink = sink_ab[li].astype(f32)
        else:
            w_in_b = w_in_c[li].astype(bf16)
            w_s_b = w_s_c[li].astype(bf16)
            bs = jnp.pad(b_s_c[li].T.astype(f32), ((0, 0), (0, LANE - C_GROUPS)))
            w_out_b = w_out_c[li].astype(bf16)
        for si, st in enumerate(streams):
            B, T = st["B"], st["T"]
            mk = [m[st["rows"], k * D:(k + 1) * D].reshape(-1, 1, D) for k in range(6)]
            x = st["x"]
            if layer % 2 == 0:
                qkv, gate, qb, kb, vb, ab, abt = ab_in_pallas(x, mk[0], mk[1], norm_g[layer, 0], w_main, w_log_pad, w_log.T)
                s0 = None if si == 0 else state_delta[:, li]
                o_a, s_fin = delta_mixer_pallas(qkv.reshape(B, T, QKV_W), gate.reshape(B, T, A_W), ab.reshape(B, T, LANE),
                                                abt, cw, prow, pcol, og, s0)
                if si == 0:
                    o_b = context_attention_pallas(qb.reshape(B, T, QB_W), kb.reshape(B, T, KV_W), vb.reshape(B, T, KV_W), sink)
                    new_k = kb.reshape(B, 1, T, B_KV_HEADS, HEAD_DIM)
                    new_v = vb.reshape(B, 1, T, B_KV_HEADS, HEAD_DIM)
                    new_s = s_fin[:, None]
                else:
                    o_b = latent_attention_pallas(qb.reshape(B, T, QB_W), kb.reshape(B, T, KV_W), vb.reshape(B, T, KV_W),
                                                  cache_attn_k[:, li].reshape(B, past, KV_W),
                                                  cache_attn_v[:, li].reshape(B, past, KV_W), sink)
                x, hn = ab_out_pallas(o_a.reshape(B * T, A_W), o_b.reshape(B * T, QB_W), w_out_b, x, mk[2], mk[3], mk[4],
                                      norm_g[layer, 1])
            else:
                x, hn = c_layer_pallas(x, mk, norm_g[layer, 0], norm_g[layer, 1], w_in_b, vnorm_c[li], w_s_b, bs, w_out_b)
            st["x"] = peer_pallas(hn, x, mk[5], wq_b, peer_keys, ed_b, eut_b, final_norm_g, last, layer)
    y_prompt = streams[0]["x"].reshape(BP, SP, D)
    y_sample = streams[1]["x"].reshape(BS, TS, D)
    return (y_prompt, y_sample, new_k, new_v, new_s)
```
